```python
import numpy as np
import jax, jax.numpy as jnp
from jax import lax

D_MODEL = 2048
BATCH = 8
SEQ = 2048
DEPTH = 2

D_CONV = D_MODEL // 4
CONV_WIDTH = 31
D_RNN = 3 * D_MODEL // 8
RNN_BLOCKS = 6
RNN_BLOCK_W = D_RNN // RNN_BLOCKS
RNN_CONV_WIDTH = 4
RG_C = 8.0
N_Q_HEADS = 6
N_KV_HEADS = 2
HEAD_DIM = 128
GROUP = N_Q_HEADS // N_KV_HEADS
D_ATTN = N_Q_HEADS * HEAD_DIM
KV_W = N_KV_HEADS * HEAD_DIM
CMP_BLOCK = 32
CMP_STRIDE = 16
SEL_BLOCK = 64
SEL_TOP_N = 16
WINDOW = 512
Q_BLOCK = 64
ROPE_THETA = 10000.0
D_FF = 4 * D_MODEL
NORM_EPS = 1e-6
NEG_INF = -1e30
POS_INF = 1e30

IN_SIZES = (D_CONV, D_CONV,
            D_RNN, D_RNN,
            D_ATTN,
            KV_W, KV_W, KV_W, KV_W, KV_W, KV_W,
            3 * N_Q_HEADS,
            D_MODEL, D_MODEL, D_MODEL)
N_IN = sum(IN_SIZES)

kernel_name = 'hybrid_conv_rglru_nsa_block'


def rms_norm(x, g):
    xf = x.astype(jnp.float32)
    y = xf * lax.rsqrt(jnp.mean(xf * xf, axis=-1, keepdims=True) + NORM_EPS)
    return (y * g).astype(x.dtype)


def layer_norm(x, g, b):
    xf = x.astype(jnp.float32)
    mu = jnp.mean(xf, axis=-1, keepdims=True)
    var = jnp.mean(jnp.square(xf - mu), axis=-1, keepdims=True)
    return ((xf - mu) * lax.rsqrt(var + NORM_EPS) * g + b).astype(x.dtype)


def masked_softmax(s, mask):
    p = jax.nn.softmax(jnp.where(mask, s, NEG_INF), axis=-1)
    return jnp.where(mask, p, 0.0)


def causal_depthwise_conv(x, w, b):
    k, c = w.shape
    y = lax.conv_general_dilated(x, w[:, None, :].astype(x.dtype), window_strides=(1,),
                                 padding=[(k - 1, 0)], dimension_numbers=('NWC', 'WIO', 'NWC'),
                                 feature_group_count=c)
    return y + b


def rope_tables(s):
    inv = 1.0 / (ROPE_THETA ** (jnp.arange(0, HEAD_DIM, 2, dtype=jnp.float32) / HEAD_DIM))
    ang = jnp.arange(s, dtype=jnp.float32)[:, None] * inv[None, :]
    return jnp.cos(ang)[:, None, :], jnp.sin(ang)[:, None, :]


def apply_rope(x, cos, sin):
    xf = x.astype(jnp.float32)
    x1, x2 = jnp.split(xf, 2, axis=-1)
    return jnp.concatenate([x1 * cos - x2 * sin, x2 * cos + x1 * sin], axis=-1).astype(x.dtype)


def _lin_combine(left, right):
    a1, b1 = left
    a2, b2 = right
    return a1 * a2, a2 * b1 + b2


def rg_lru(x, wa, ba, wx, bx, lam):
    b, s, d = x.shape
    xb = x.reshape(b, s, RNN_BLOCKS, RNN_BLOCK_W)
    r = jax.nn.sigmoid((jnp.einsum('bsnc,ncd->bsnd', xb, wa).reshape(b, s, d) + ba).astype(jnp.float32))
    i = jax.nn.sigmoid((jnp.einsum('bsnc,ncd->bsnd', xb, wx).reshape(b, s, d) + bx).astype(jnp.float32))
    log_a = -RG_C * jax.nn.softplus(-lam.astype(jnp.float32)) * r
    a = jnp.exp(log_a)
    gated_x = jnp.sqrt(-jnp.expm1(2.0 * log_a)) * (i * x.astype(jnp.float32))
    _, h = lax.associative_scan(_lin_combine, (a, gated_x), axis=1)
    return h.astype(x.dtype)


def nsa_attention(q, k_cmp, v_cmp, k_slc, v_slc, k_win, v_win, gates, cos, sin,
                  cmp_pe, cmp_k_w1, cmp_k_w2, cmp_v_w1, cmp_v_w2):
    b, s = q.shape[:2]
    scale = HEAD_DIM ** -0.5
    q_rot = apply_rope(q, cos, sin)
    k_slc = apply_rope(k_slc, cos, sin)
    k_win = apply_rope(k_win, cos, sin)

    n_cmp = (s - CMP_BLOCK) // CMP_STRIDE + 1
    cmp_idx = np.arange(n_cmp)[:, None] * CMP_STRIDE + np.arange(CMP_BLOCK)[None, :]

    def compress(kv, w1, w2):
        blk = kv[:, cmp_idx] + cmp_pe[None, None, :, None, :]
        blk = blk.transpose(0, 1, 3, 2, 4).reshape(b, n_cmp, N_KV_HEADS, CMP_BLOCK * HEAD_DIM)
        return jax.nn.gelu(blk @ w1) @ w2

    kc = compress(k_cmp, cmp_k_w1, cmp_k_w2)
    vc = compress(v_cmp, cmp_v_w1, cmp_v_w2)
    cmp_end = jnp.asarray(cmp_idx[:, -1], dtype=jnp.int32)

    n_sel = s // SEL_BLOCK
    n_top = min(SEL_TOP_N, n_sel)
    c_start = np.arange(n_cmp) * CMP_STRIDE
    s_start = np.arange(n_sel) * SEL_BLOCK
    overlap = jnp.asarray(((c_start[:, None] < s_start[None, :] + SEL_BLOCK)
                           & (c_start[:, None] + CMP_BLOCK > s_start[None, :])).astype(np.float32))
    ksb = k_slc.reshape(b, n_sel, SEL_BLOCK, N_KV_HEADS, HEAD_DIM).transpose(0, 3, 1, 2, 4)
    vsb = v_slc.reshape(b, n_sel, SEL_BLOCK, N_KV_HEADS, HEAD_DIM).transpose(0, 3, 1, 2, 4)
    gather = jax.vmap(jax.vmap(lambda blocks, idx: blocks[idx]))

    kw_pad = jnp.pad(k_win, ((0, 0), (WINDOW, 0), (0, 0), (0, 0)))
    vw_pad = jnp.pad(v_win, ((0, 0), (WINDOW, 0), (0, 0), (0, 0)))

    nq = s // Q_BLOCK

    def to_blocks(a):
        a = a.reshape(b, nq, Q_BLOCK, N_KV_HEADS, GROUP, a.shape[-1])
        return jnp.moveaxis(a, 1, 0)

    def block_fn(args):
        c, qn, qr, g = args
        t = c * Q_BLOCK + jnp.arange(Q_BLOCK, dtype=jnp.int32)
        s_c = jnp.einsum('bqhgd,bnhd->bhgqn', qn, kc).astype(jnp.float32) * scale
        p_c = masked_softmax(s_c, cmp_end[None, :] <= t[:, None])
        o_c = jnp.einsum('bhgqn,bnhd->bqhgd', p_c.astype(vc.dtype), vc)
        imp = jnp.einsum('bhgqn,nm->bhqm', p_c, overlap)
        blk = jnp.arange(n_sel, dtype=jnp.int32)[None, :]
        cur = (t // SEL_BLOCK)[:, None]
        valid = blk * SEL_BLOCK <= t[:, None]
        forced = (blk == 0) | (blk == cur) | (blk == cur - 1)
        score = jnp.where(valid, jnp.where(forced, POS_INF, imp), NEG_INF)
        _, idx = lax.top_k(score, n_top)
        kg = gather(ksb, idx)
        vg = gather(vsb, idx)
        kpos = idx[..., None] * SEL_BLOCK + jnp.arange(SEL_BLOCK, dtype=jnp.int32)
        m_s = (kpos <= t[:, None, None]).reshape(b, N_KV_HEADS, 1, Q_BLOCK, n_top * SEL_BLOCK)
        s_s = jnp.einsum('bqhgd,bhqnkd->bhgqnk', qr, kg).astype(jnp.float32) * scale
        p_s = masked_softmax(s_s.reshape(b, N_KV_HEADS, GROUP, Q_BLOCK, n_top * SEL_BLOCK), m_s)
        o_s = jnp.einsum('bhgqm,bhqmd->bqhgd', p_s.astype(vg.dtype),
                         vg.reshape(b, N_KV_HEADS, Q_BLOCK, n_top * SEL_BLOCK, HEAD_DIM))
        kw = lax.dynamic_slice_in_dim(kw_pad, c * Q_BLOCK, WINDOW + Q_BLOCK, axis=1)
        vw = lax.dynamic_slice_in_dim(vw_pad, c * Q_BLOCK, WINDOW + Q_BLOCK, axis=1)
        kpos_w = c * Q_BLOCK - WINDOW + jnp.arange(WINDOW + Q_BLOCK, dtype=jnp.int32)
        diff = t[:, None] - kpos_w[None, :]
        m_w = (diff >= 0) & (diff < WINDOW) & (kpos_w[None, :] >= 0)
        s_w = jnp.einsum('bqhgd,bkhd->bhgqk', qr, kw).astype(jnp.float32) * scale
        p_w = masked_softmax(s_w, m_w)
        o_w = jnp.einsum('bhgqk,bkhd->bqhgd', p_w.astype(vw.dtype), vw)
        return g[..., 0:1] * o_c + g[..., 1:2] * o_s + g[..., 2:3] * o_w

    o = lax.map(block_fn, (jnp.arange(nq, dtype=jnp.int32), to_blocks(q), to_blocks(q_rot), to_blocks(gates)))
    return jnp.moveaxis(o, 0, 1).reshape(b, s, D_ATTN)


def setup_inputs(seed: int = 0) -> dict:
    key = jax.random.key(seed)
    ks = jax.random.split(key, 32)
    L = DEPTH
    f32 = jnp.float32

    def nrm(k, shape, fan_in):
        return jax.random.normal(k, shape, f32) * (fan_in ** -0.5)

    def gain(k, shape):
        return 1.0 + 0.02 * jax.random.normal(k, shape, f32)

    def bias(k, shape):
        return 0.02 * jax.random.normal(k, shape, f32)

    u = jax.random.uniform(ks[14], (L, D_RNN), f32, minval=0.9, maxval=0.999)
    sa = u ** (1.0 / RG_C)
    return {
        'x': jax.random.normal(ks[0], (BATCH, SEQ, D_MODEL), f32),
        'attn_norm_g': gain(ks[1], (L, D_MODEL)),
        'w_in': nrm(ks[2], (L, D_MODEL, N_IN), D_MODEL),
        'conv_dw_w': nrm(ks[3], (L, CONV_WIDTH, D_CONV), CONV_WIDTH),
        'conv_dw_b': bias(ks[4], (L, D_CONV)),
        'conv_ln_g': gain(ks[5], (L, D_CONV)),
        'conv_ln_b': bias(ks[6], (L, D_CONV)),
        'w_conv_out': nrm(ks[7], (L, D_CONV, D_MODEL), D_CONV),
        'rnn_conv_w': nrm(ks[8], (L, RNN_CONV_WIDTH, D_RNN), RNN_CONV_WIDTH),
        'rnn_conv_b': bias(ks[9], (L, D_RNN)),
        'rglru_wa': nrm(ks[10], (L, RNN_BLOCKS, RNN_BLOCK_W, RNN_BLOCK_W), RNN_BLOCK_W),
        'rglru_ba': bias(ks[11], (L, D_RNN)),
        'rglru_wx': nrm(ks[12], (L, RNN_BLOCKS, RNN_BLOCK_W, RNN_BLOCK_W), RNN_BLOCK_W),
        'rglru_bx': bias(ks[13], (L, D_RNN)),
        'rglru_lambda': jnp.log(sa) - jnp.log1p(-sa),
        'w_rnn_out': nrm(ks[15], (L, D_RNN, D_MODEL), D_RNN),
        'cmp_pe': 0.02 * jax.random.normal(ks[16], (L, CMP_BLOCK, HEAD_DIM), f32),
        'cmp_k_w1': nrm(ks[17], (L, CMP_BLOCK * HEAD_DIM, HEAD_DIM), CMP_BLOCK * HEAD_DIM),
        'cmp_k_w2': nrm(ks[18], (L, HEAD_DIM, HEAD_DIM), HEAD_DIM),
        'cmp_v_w1': nrm(ks[19], (L, CMP_BLOCK * HEAD_DIM, HEAD_DIM), CMP_BLOCK * HEAD_DIM),
        'cmp_v_w2': nrm(ks[20], (L, HEAD_DIM, HEAD_DIM), HEAD_DIM),
        'w_attn_out': nrm(ks[21], (L, D_ATTN, D_MODEL), D_ATTN),
        'w_o': nrm(ks[22], (L, D_MODEL, D_MODEL), D_MODEL),
        'mlp_norm_g': gain(ks[23], (L, D_MODEL)),
        'w_mlp_up': nrm(ks[24], (L, D_MODEL, D_FF), D_MODEL),
        'w_mlp_down': nrm(ks[25], (L, D_FF, D_MODEL), D_FF),
        'final_norm_g': gain(ks[26], (D_MODEL,)),
    }


def reference(x, attn_norm_g, w_in, conv_dw_w, conv_dw_b, conv_ln_g, conv_ln_b, w_conv_out,
              rnn_conv_w, rnn_conv_b, rglru_wa, rglru_ba, rglru_wx, rglru_bx, rglru_lambda, w_rnn_out,
              cmp_pe, cmp_k_w1, cmp_k_w2, cmp_v_w1, cmp_v_w2, w_attn_out,
              w_o, mlp_norm_g, w_mlp_up, w_mlp_down, final_norm_g):
    b, s, _ = x.shape
    cos, sin = rope_tables(s)
    split_points = [int(v) for v in np.cumsum(IN_SIZES)[:-1]]

    def kv_heads(z):
        return z.reshape(b, s, N_KV_HEADS, HEAD_DIM)

    for l in range(DEPTH):
        h = rms_norm(x, attn_norm_g[l])
        proj = h @ w_in[l]
        (a_val, a_gate, r_x, r_gate, c_q, c_kc, c_vc, c_ks, c_vs, c_kw, c_vw, c_g,
         g_a, g_b, g_c) = jnp.split(proj, split_points, axis=-1)

        u = a_val * jax.nn.sigmoid(a_gate)
        u = causal_depthwise_conv(u, conv_dw_w[l], conv_dw_b[l])
        u = jax.nn.silu(layer_norm(u, conv_ln_g[l], conv_ln_b[l]))
        p_a = u @ w_conv_out[l]

        r = causal_depthwise_conv(r_x, rnn_conv_w[l], rnn_conv_b[l])
        r = rg_lru(r, rglru_wa[l], rglru_ba[l], rglru_wx[l], rglru_bx[l], rglru_lambda[l])
        p_b = (r * jax.nn.gelu(r_gate)) @ w_rnn_out[l]

        o = nsa_attention(c_q.reshape(b, s, N_Q_HEADS, HEAD_DIM), kv_heads(c_kc), kv_heads(c_vc),
                          kv_heads(c_ks), kv_heads(c_vs), kv_heads(c_kw), kv_heads(c_vw),
                          jax.nn.sigmoid(c_g).reshape(b, s, N_Q_HEADS, 3), cos, sin,
                          cmp_pe[l], cmp_k_w1[l], cmp_k_w2[l], cmp_v_w1[l], cmp_v_w2[l])
        p_c = o @ w_attn_out[l]

        y = jax.nn.sigmoid(g_a) * p_a + jax.nn.sigmoid(g_b) * p_b + jax.nn.sigmoid(g_c) * p_c
        x = x + y @ w_o[l]

        h2 = rms_norm(x, mlp_norm_g[l])
        x = x + jnp.square(jax.nn.relu(h2 @ w_mlp_up[l])) @ w_mlp_down[l]

    return rms_norm(x, final_norm_g)
```

```python
import numpy as np
import jax
import jax.numpy as jnp
from jax import lax
from jax.experimental import pallas as pl
from jax.experimental.pallas import tpu as pltpu

F32 = jnp.float32
BF16 = jnp.bfloat16

D_MODEL = 2048
BATCH = 8
SEQ = 2048
DEPTH = 2

D_CONV = D_MODEL // 4
CONV_WIDTH = 31
D_RNN = 3 * D_MODEL // 8
RNN_BLOCKS = 6
RNN_BLOCK_W = D_RNN // RNN_BLOCKS
RNN_CONV_WIDTH = 4
RG_C = 8.0
N_Q_HEADS = 6
N_KV_HEADS = 2
HEAD_DIM = 128
GROUP = N_Q_HEADS // N_KV_HEADS
D_ATTN = N_Q_HEADS * HEAD_DIM
KV_W = N_KV_HEADS * HEAD_DIM
CMP_BLOCK = 32
CMP_STRIDE = 16
SEL_BLOCK = 64
SEL_TOP_N = 16
WINDOW = 512
ROPE_THETA = 10000.0
D_FF = 4 * D_MODEL
NORM_EPS = 1e-6
NEG_INF = -1e30
POS_INF = 1e30

N_CMP_PAD = SEQ // CMP_STRIDE
N_SEL = SEQ // SEL_BLOCK

LANES = 128
SLAB = 256
SL_GA, SL_GB, SL_GC = 0, 8, 16
SL_RX, SL_RG = 24, 27
SL_Q = 30
SL_KC, SL_VC, SL_KS = 33, 34, 35
SL_AV, SL_AG = 36, 38
SL_VS, SL_KW, SL_VW, SL_CG = 40, 41, 42, 43
N_SLABS = 44
N_IN_PAD = N_SLABS * SLAB

_IN_SIZES = (D_CONV, D_CONV, D_RNN, D_RNN, D_ATTN, KV_W, KV_W, KV_W, KV_W, KV_W, KV_W,
             3 * N_Q_HEADS, D_MODEL, D_MODEL, D_MODEL)
_IN_OFF = np.concatenate([[0], np.cumsum(_IN_SIZES)])
(_O_AV, _O_AG, _O_RX, _O_RG, _O_Q, _O_KC, _O_VC, _O_KS, _O_VS, _O_KW, _O_VW, _O_CG,
 _O_GA, _O_GB, _O_GC) = [int(v) for v in _IN_OFF[:-1]]

VMEM_LIMIT = 56 * 1024 * 1024


def _cparams(sem, vmem=VMEM_LIMIT):
    return pltpu.CompilerParams(dimension_semantics=sem, vmem_limit_bytes=vmem)


def _sigmoid(x):
    return 1.0 / (1.0 + jnp.exp(-x))


def _gelu_tanh(x):
    c = np.float32(np.sqrt(2.0 / np.pi))
    return 0.5 * x * (1.0 + jnp.tanh(c * (x + 0.044715 * (x * x * x))))


IN_TM = 1024
IN_TN = 512
NORM_RC = 128


def _rmsnorm_to(h_ref, x_ref, g_ref, rows):
    g = g_ref[...]

    def body(c, carry):
        r0 = pl.multiple_of(c * NORM_RC, NORM_RC)
        x = x_ref[pl.ds(r0, NORM_RC), :]
        ms = jnp.mean(x * x, axis=-1, keepdims=True)
        h_ref[pl.ds(r0, NORM_RC), :] = (x * lax.rsqrt(ms + NORM_EPS) * g).astype(h_ref.dtype)
        return carry

    lax.fori_loop(0, rows // NORM_RC, body, 0)


def _inproj_kernel(x_ref, g_ref, w_ref, o_ref, h_ref):
    @pl.when(pl.program_id(1) == 0)
    def _():
        _rmsnorm_to(h_ref, x_ref, g_ref, IN_TM)

    r = jnp.dot(h_ref[...], w_ref[...], preferred_element_type=F32)
    for k in range(IN_TN // SLAB):
        o_ref[k] = r[:, k * SLAB:(k + 1) * SLAB].astype(o_ref.dtype)


def _in_projection(x2, g, w_perm):
    rows, d = x2.shape
    return pl.pallas_call(
        _inproj_kernel,
        grid=(rows // IN_TM, N_IN_PAD // IN_TN),
        in_specs=[pl.BlockSpec((IN_TM, d), lambda i, j: (i, 0)),
                  pl.BlockSpec((1, d), lambda i, j: (0, 0)),
                  pl.BlockSpec((d, IN_TN), lambda i, j: (0, j))],
        out_specs=pl.BlockSpec((IN_TN // SLAB, IN_TM, SLAB), lambda i, j: (j, i, 0)),
        out_shape=jax.ShapeDtypeStruct((N_SLABS, rows, SLAB), BF16),
        scratch_shapes=[pltpu.VMEM((IN_TM, d), BF16)],
        compiler_params=_cparams(("parallel", "arbitrary")),
        name="in_projection",
    )(x2, g.reshape(1, d), w_perm)


MIX_TS = 256
MIX_TR = MIX_TS * BATCH
CONV_HALO = 256
CONV_RC = 64


def _conv_kernel(v_ref, g_ref, w_ref, b_ref, lg_ref, lb_ref, o_ref, ubuf, ybuf):
    nc = D_CONV // LANES
    per_slab = SLAB // LANES

    @pl.when(pl.program_id(0) == 0)
    def _():
        ubuf[:, 0:CONV_HALO, :] = jnp.zeros((nc, CONV_HALO, LANES), F32)

    def glu(b, carry):
        for c in range(nc):
            k, ls = c // per_slab, slice((c % per_slab) * LANES, (c % per_slab + 1) * LANES)
            v = v_ref[k, b, :, ls].astype(F32)
            g = g_ref[k, b, :, ls].astype(F32)
            ubuf[c, pl.ds(CONV_HALO + b, MIX_TS, stride=BATCH), :] = v * _sigmoid(g)
        return carry

    lax.fori_loop(0, BATCH, glu, 0)

    base = CONV_HALO - (CONV_WIDTH - 1) * BATCH

    def conv(i, carry):
        r0 = pl.multiple_of(i * CONV_RC, CONV_RC)
        acc = []
        for c in range(nc):
            ls = slice(c * LANES, (c + 1) * LANES)
            a = jnp.zeros((CONV_RC, LANES), F32)
            for j in range(CONV_WIDTH):
                a = a + w_ref[j:j + 1, ls] * ubuf[c, pl.ds(r0 + base + BATCH * j, CONV_RC), :]
            acc.append(a + b_ref[:, ls])
        mu = sum(jnp.sum(a, axis=-1, keepdims=True) for a in acc) * (1.0 / D_CONV)
        cen = [a - mu for a in acc]
        var = sum(jnp.sum(a * a, axis=-1, keepdims=True) for a in cen) * (1.0 / D_CONV)
        inv = lax.rsqrt(var + NORM_EPS)
        for c in range(nc):
            ls = slice(c * LANES, (c + 1) * LANES)
            y = cen[c] * inv * lg_ref[:, ls] + lb_ref[:, ls]
            ybuf[c, pl.ds(r0, CONV_RC), :] = y * _sigmoid(y)
        return carry

    lax.fori_loop(0, MIX_TR // CONV_RC, conv, 0)

    def put(b, carry):
        for c in range(nc):
            k, ls = c // per_slab, slice((c % per_slab) * LANES, (c % per_slab + 1) * LANES)
            o_ref[k, b, :, ls] = ybuf[c, pl.ds(b, MIX_TS, stride=BATCH), :].astype(o_ref.dtype)
        return carry

    lax.fori_loop(0, BATCH, put, 0)
    ubuf[:, 0:CONV_HALO, :] = ubuf[:, MIX_TR:MIX_TR + CONV_HALO, :]


def _conv_branch(proj, w, b, lg, lb):
    seq = proj.shape[1] // BATCH
    p4 = proj.reshape(N_SLABS, BATCH, seq, SLAB)
    nk = D_CONV // SLAB
    vec = lambda: pl.BlockSpec((1, D_CONV), lambda i: (0, 0))
    out = pl.pallas_call(
        _conv_kernel,
        grid=(seq // MIX_TS,),
        in_specs=[pl.BlockSpec((nk, BATCH, MIX_TS, SLAB), lambda i: (SL_AV // nk, 0, i, 0)),
                  pl.BlockSpec((nk, BATCH, MIX_TS, SLAB), lambda i: (SL_AG // nk, 0, i, 0)),
                  pl.BlockSpec((CONV_WIDTH, D_CONV), lambda i: (0, 0)),
                  vec(), vec(), vec()],
        out_specs=pl.BlockSpec((nk, BATCH, MIX_TS, SLAB), lambda i: (0, 0, i, 0)),
        out_shape=jax.ShapeDtypeStruct((nk, BATCH, seq, SLAB), BF16),
        scratch_shapes=[pltpu.VMEM((D_CONV // LANES, CONV_HALO + MIX_TR, LANES), F32),
                        pltpu.VMEM((D_CONV // LANES, MIX_TR, LANES), F32)],
        compiler_params=_cparams(("arbitrary",)),
        name="conv_branch",
    )(p4, p4, w, b.reshape(1, -1), lg.reshape(1, -1), lb.reshape(1, -1))
    return out.reshape(nk, BATCH * seq, SLAB)


RNN_HALO = 32
RNN_RC = 256


def _rglru_kernel(x_ref, gate_ref, cw_ref, cb_ref, wa_ref, ba_ref, wx_ref, bx_ref, lam_ref,
                  o_ref, xbuf, abuf, gbuf, hstate):
    per_slab = SLAB // RNN_BLOCK_W

    @pl.when(pl.program_id(0) == 0)
    def _():
        xbuf[:, 0:RNN_HALO, :] = jnp.zeros((RNN_BLOCKS, RNN_HALO, RNN_BLOCK_W), F32)
        hstate[...] = jnp.zeros_like(hstate)

    def slab_cols(n):
        return n // per_slab, slice((n % per_slab) * RNN_BLOCK_W, (n % per_slab + 1) * RNN_BLOCK_W)

    def load(b, carry):
        for n in range(RNN_BLOCKS):
            k, ls = slab_cols(n)
            xbuf[n, pl.ds(RNN_HALO + b, MIX_TS, stride=BATCH), :] = x_ref[k, b, :, ls].astype(F32)
        return carry

    lax.fori_loop(0, BATCH, load, 0)

    z = -lam_ref[...]
    softplus = jnp.maximum(z, 0.0) + jnp.log(1.0 + jnp.exp(-jnp.abs(z)))
    coef = -RG_C * softplus
    base = RNN_HALO - (RNN_CONV_WIDTH - 1) * BATCH

    def gates(i, carry):
        r0 = pl.multiple_of(i * RNN_RC, RNN_RC)
        for n in range(RNN_BLOCKS):
            cs = slice(n * RNN_BLOCK_W, (n + 1) * RNN_BLOCK_W)
            y = jnp.zeros((RNN_RC, RNN_BLOCK_W), F32)
            for j in range(RNN_CONV_WIDTH):
                y = y + cw_ref[j:j + 1, cs] * xbuf[n, pl.ds(r0 + base + BATCH * j, RNN_RC), :]
            y = y + cb_ref[:, cs]
            yb = y.astype(BF16)
            ra = _sigmoid(jnp.dot(yb, wa_ref[n], preferred_element_type=F32) + ba_ref[:, cs])
            ri = _sigmoid(jnp.dot(yb, wx_ref[n], preferred_element_type=F32) + bx_ref[:, cs])
            a = jnp.exp(coef[:, cs] * ra)
            abuf[n, pl.ds(r0, RNN_RC), :] = a
            gbuf[n, pl.ds(r0, RNN_RC), :] = jnp.sqrt(1.0 - a * a) * (ri * y)
        return carry

    lax.fori_loop(0, MIX_TR // RNN_RC, gates, 0)

    def step(t, h):
        r0 = pl.multiple_of(t * BATCH, BATCH)
        h = abuf[:, pl.ds(r0, BATCH), :] * h + gbuf[:, pl.ds(r0, BATCH), :]
        gbuf[:, pl.ds(r0, BATCH), :] = h
        return h

    hstate[...] = lax.fori_loop(0, MIX_TS, step, hstate[...], unroll=8)

    def put(b, carry):
        for n in range(RNN_BLOCKS):
            k, ls = slab_cols(n)
            h = gbuf[n, pl.ds(b, MIX_TS, stride=BATCH), :]
            o_ref[k, b, :, ls] = (h * _gelu_tanh(gate_ref[k, b, :, ls].astype(F32))).astype(o_ref.dtype)
        return carry

    lax.fori_loop(0, BATCH, put, 0)
    xbuf[:, 0:RNN_HALO, :] = xbuf[:, MIX_TR:MIX_TR + RNN_HALO, :]


def _rglru_branch(proj, cw, cb, wa, ba, wx, bx, lam):
    seq = proj.shape[1] // BATCH
    p4 = proj.reshape(N_SLABS, BATCH, seq, SLAB)
    nk = D_RNN // SLAB
    vec = lambda: pl.BlockSpec((1, D_RNN), lambda i: (0, 0))
    blk = lambda: pl.BlockSpec((RNN_BLOCKS, RNN_BLOCK_W, RNN_BLOCK_W), lambda i: (0, 0, 0))
    out = pl.pallas_call(
        _rglru_kernel,
        grid=(seq // MIX_TS,),
        in_specs=[pl.BlockSpec((nk, BATCH, MIX_TS, SLAB), lambda i: (SL_RX // nk, 0, i, 0)),
                  pl.BlockSpec((nk, BATCH, MIX_TS, SLAB), lambda i: (SL_RG // nk, 0, i, 0)),
                  pl.BlockSpec((RNN_CONV_WIDTH, D_RNN), lambda i: (0, 0)),
                  vec(), blk(), vec(), blk(), vec(), vec()],
        out_specs=pl.BlockSpec((nk, BATCH, MIX_TS, SLAB), lambda i: (0, 0, i, 0)),
        out_shape=jax.ShapeDtypeStruct((nk, BATCH, seq, SLAB), BF16),
        scratch_shapes=[pltpu.VMEM((RNN_BLOCKS, RNN_HALO + MIX_TR, RNN_BLOCK_W), F32),
                        pltpu.VMEM((RNN_BLOCKS, MIX_TR, RNN_BLOCK_W), F32),
                        pltpu.VMEM((RNN_BLOCKS, MIX_TR, RNN_BLOCK_W), F32),
                        pltpu.VMEM((RNN_BLOCKS, BATCH, RNN_BLOCK_W), F32)],
        compiler_params=_cparams(("arbitrary",)),
        name="rglru_branch",
    )(p4, p4, cw, cb.reshape(1, -1), wa.astype(BF16), ba.reshape(1, -1),
      wx.astype(BF16), bx.reshape(1, -1), lam.reshape(1, -1))
    return out.reshape(nk, BATCH * seq, SLAB)


def _compress_kernel(k_ref, v_ref, pe_ref, kw1_ref, vw1_ref, kw2_ref, vw2_ref, kc_ref, vc_ref,
                     stage, acc, pacc):
    for src, w1_ref, w2_ref, dst in ((k_ref, kw1_ref, kw2_ref, kc_ref),
                                     (v_ref, vw1_ref, vw2_ref, vc_ref)):
        for h in range(N_KV_HEADS):
            stage[h] = src[:, h * HEAD_DIM:(h + 1) * HEAD_DIM].astype(F32)
        acc[...] = jnp.zeros_like(acc)
        pacc[...] = jnp.zeros_like(pacc)

        def body(l, carry):
            w1 = w1_ref[l]
            for h in range(N_KV_HEADS):
                x = stage[h, pl.ds(l, N_CMP_PAD, stride=CMP_STRIDE), :]
                acc[h] += jnp.dot(x.astype(BF16), w1, preferred_element_type=F32)
            pe = pe_ref[l].astype(BF16)
            pacc[...] += (jnp.dot(pe[:, :HEAD_DIM], w1[:, :HEAD_DIM], preferred_element_type=F32)
                          + jnp.dot(pe[:, HEAD_DIM:], w1[:, HEAD_DIM:], preferred_element_type=F32))
            return carry

        lax.fori_loop(0, CMP_STRIDE, body, 0)

        for h in range(N_KV_HEADS):
            p = acc[h]
            hi_next = pltpu.roll(p[:, HEAD_DIM:], N_CMP_PAD - 1, 0)
            pre = p[:, :HEAD_DIM] + hi_next + pacc[0:1, :]
            y = jnp.dot(_gelu_tanh(pre).astype(BF16), w2_ref[...], preferred_element_type=F32)
            dst[:, h * HEAD_DIM:(h + 1) * HEAD_DIM] = y.astype(dst.dtype)


def _compress(proj, pe, kw1, kw2, vw1, vw2):
    seq = proj.shape[1] // BATCH
    pe2 = jnp.zeros((CMP_STRIDE, 8, 2 * HEAD_DIM), F32)
    pe2 = pe2.at[:, 0, :HEAD_DIM].set(pe[:CMP_STRIDE]).at[:, 0, HEAD_DIM:].set(pe[CMP_STRIDE:])

    def w1cat(w1):
        w = w1.reshape(2, CMP_STRIDE, HEAD_DIM, HEAD_DIM)
        return jnp.concatenate([w[0], w[1]], axis=-1).astype(BF16)

    kv_spec = lambda sl: pl.BlockSpec((None, seq, SLAB), lambda b: (sl, b, 0))
    w1_spec = lambda: pl.BlockSpec((CMP_STRIDE, HEAD_DIM, 2 * HEAD_DIM), lambda b: (0, 0, 0))
    w2_spec = lambda: pl.BlockSpec((HEAD_DIM, HEAD_DIM), lambda b: (0, 0))
    out_spec = lambda: pl.BlockSpec((None, N_CMP_PAD, KV_W), lambda b: (b, 0, 0))
    return pl.pallas_call(
        _compress_kernel,
        grid=(BATCH,),
        in_specs=[kv_spec(SL_KC), kv_spec(SL_VC),
                  pl.BlockSpec((CMP_STRIDE, 8, 2 * HEAD_DIM), lambda b: (0, 0, 0)),
                  w1_spec(), w1_spec(), w2_spec(), w2_spec()],
        out_specs=[out_spec(), out_spec()],
        out_shape=[jax.ShapeDtypeStruct((BATCH, N_CMP_PAD, KV_W), BF16)] * 2,
        scratch_shapes=[pltpu.VMEM((N_KV_HEADS, seq, HEAD_DIM), F32),
                        pltpu.VMEM((N_KV_HEADS, N_CMP_PAD, 2 * HEAD_DIM), F32),
                        pltpu.VMEM((8, HEAD_DIM), F32)],
        compiler_params=_cparams(("parallel",)),
        name="compress_kv",
    )(proj, proj, pe2, w1cat(kw1), w1cat(vw1), kw2.astype(BF16), vw2.astype(BF16))


ATT_TQ = 256
ATT_TK = 256
ATT_SPAN = WINDOW + ATT_TQ


def _nt_dot(a, b):
    return lax.dot_general(a, b, (((1,), (1,)), ((), ())), preferred_element_type=F32)


def _attn_kernel(q_ref, ks_ref, vs_ref, kw_ref, vw_ref, kc_ref, vc_ref, cg_ref, cos_ref, sin_ref,
                 ovt_ref, exp_ref, o_ref, ksr, kwr, bias_scr):
    qi = pl.program_id(1)
    scale = np.float32(HEAD_DIM ** -0.5)

    @pl.when(qi == 0)
    def _():
        def rope_k(c, carry):
            r0 = pl.multiple_of(c * ATT_TK, ATT_TK)
            cos = cos_ref[pl.ds(r0, ATT_TK), :]
            sin = sin_ref[pl.ds(r0, ATT_TK), :]
            for src, dst in ((ks_ref, ksr), (kw_ref, kwr)):
                for h in range(N_KV_HEADS):
                    cs = slice(h * HEAD_DIM, (h + 1) * HEAD_DIM)
                    x = src[pl.ds(r0, ATT_TK), cs].astype(F32)
                    dst[pl.ds(r0, ATT_TK), cs] = (
                        x * cos + pltpu.roll(x, HEAD_DIM // 2, 1) * sin).astype(dst.dtype)
            return carry

        lax.fori_loop(0, SEQ // ATT_TK, rope_k, 0)

    t0 = pl.multiple_of(qi * ATT_TQ, ATT_TQ)
    cos_q = cos_ref[pl.ds(t0, ATT_TQ), :]
    sin_q = sin_ref[pl.ds(t0, ATT_TQ), :]
    gate = _sigmoid(cg_ref[:, 0:HEAD_DIM].astype(F32))

    t_lane = t0 + lax.broadcasted_iota(jnp.int32, (N_CMP_PAD, ATT_TQ), 1)
    n_sub = lax.broadcasted_iota(jnp.int32, (N_CMP_PAD, ATT_TQ), 0)
    valid_c = (n_sub * CMP_STRIDE + (CMP_BLOCK - 1)) <= t_lane
    t_lane_s = t0 + lax.broadcasted_iota(jnp.int32, (N_SEL, ATT_TQ), 1)
    m_sub = lax.broadcasted_iota(jnp.int32, (N_SEL, ATT_TQ), 0)
    cur = t_lane_s >> 6
    valid_s = m_sub <= cur
    forced = (m_sub == 0) | (m_sub == cur) | (m_sub == cur - 1)
    t_row = t0 + lax.broadcasted_iota(jnp.int32, (ATT_TQ, ATT_TK), 0)
    k_col = lax.broadcasted_iota(jnp.int32, (ATT_TQ, ATT_TK), 1)

    w_start = pl.multiple_of(jnp.maximum(t0 - WINDOW, 0), ATT_TQ)
    t_row_w = t0 + lax.broadcasted_iota(jnp.int32, (ATT_TQ, ATT_SPAN), 0)
    dist_w = t_row_w - (w_start + lax.broadcasted_iota(jnp.int32, (ATT_TQ, ATT_SPAN), 1))
    valid_w = (dist_w >= 0) & (dist_w < WINDOW)

    for h in range(N_KV_HEADS):
        hs = slice(h * HEAD_DIM, (h + 1) * HEAD_DIM)
        kc = kc_ref[:, hs]
        vc = vc_ref[:, hs]
        q_plain, q_rot = [], []
        for g in range(GROUP):
            hq = h * GROUP + g
            xq = q_ref[hq // 2, :, (hq % 2) * HEAD_DIM:(hq % 2 + 1) * HEAD_DIM].astype(F32)
            q_plain.append((xq * scale).astype(BF16))
            q_rot.append(((xq * cos_q + pltpu.roll(xq, HEAD_DIM // 2, 1) * sin_q) * scale).astype(BF16))

        o_cmp = []
        p_sum = jnp.zeros((N_CMP_PAD, ATT_TQ), F32)
        for g in range(GROUP):
            s = jnp.where(valid_c, _nt_dot(kc, q_plain[g]), NEG_INF)
            mx = jnp.max(s, axis=0, keepdims=True)
            e = jnp.where(valid_c, jnp.exp(s - mx), 0.0)
            den = jnp.sum(e, axis=0, keepdims=True)
            p = e / jnp.where(den > 0.0, den, 1.0)
            p_sum = p_sum + p
            o_cmp.append(jnp.dot(p.T.astype(BF16), vc, preferred_element_type=F32))

        imp = jnp.dot(ovt_ref[...], p_sum, preferred_element_type=F32,
                      precision=lax.Precision.HIGHEST)
        score = jnp.where(valid_s, jnp.where(forced, POS_INF, imp), NEG_INF)
        rank = jnp.zeros((N_SEL, ATT_TQ), F32)
        for mp in range(N_SEL):
            row = score[mp:mp + 1, :]
            beats = (row > score) | ((row == score) & (m_sub > mp))
            rank = rank + jnp.where(beats, 1.0, 0.0)
        sel = jnp.where((rank < SEL_TOP_N) & valid_s, 1.0, 0.0)
        sel_keys = jnp.dot(sel.T.astype(BF16), exp_ref[...], preferred_element_type=F32)
        for j in range(SEQ // ATT_TK):
            allowed = (sel_keys[:, j * ATT_TK:(j + 1) * ATT_TK] > 0.5) & ((k_col + j * ATT_TK) <= t_row)
            bias_scr[j] = jnp.where(allowed, 0.0, NEG_INF)

        for g in range(GROUP):
            hq = h * GROUP + g
            qr = q_rot[g]

            def sel_body(j, carry):
                m_run, l_run, acc = carry
                k0 = pl.multiple_of(j * ATT_TK, ATT_TK)
                s = _nt_dot(qr, ksr[pl.ds(k0, ATT_TK), hs]) + bias_scr[j]
                m_new = jnp.maximum(m_run, jnp.max(s, axis=-1, keepdims=True))
                alpha = jnp.exp(m_run - m_new)
                p = jnp.exp(s - m_new)
                l_new = alpha * l_run + jnp.sum(p, axis=-1, keepdims=True)
                acc = alpha * acc + jnp.dot(p.astype(BF16), vs_ref[pl.ds(k0, ATT_TK), hs],
                                            preferred_element_type=F32)
                return m_new, l_new, acc

            m0 = jnp.full((ATT_TQ, 1), NEG_INF, F32)
            l0 = jnp.zeros((ATT_TQ, 1), F32)
            a0 = jnp.zeros((ATT_TQ, HEAD_DIM), F32)
            _, l_s, acc_s = lax.fori_loop(0, qi + 1, sel_body, (m0, l0, a0))
            o_sel = acc_s / l_s

            s = jnp.where(valid_w, _nt_dot(qr, kwr[pl.ds(w_start, ATT_SPAN), hs]), NEG_INF)
            mx = jnp.max(s, axis=-1, keepdims=True)
            p = jnp.exp(s - mx)
            den = jnp.sum(p, axis=-1, keepdims=True)
            o_win = jnp.dot(p.astype(BF16), vw_ref[pl.ds(w_start, ATT_SPAN), hs],
                            preferred_element_type=F32) / den

            out = (gate[:, 3 * hq:3 * hq + 1] * o_cmp[g] + gate[:, 3 * hq + 1:3 * hq + 2] * o_sel
                   + gate[:, 3 * hq + 2:3 * hq + 3] * o_win)
            o_ref[hq // 2, :, (hq % 2) * HEAD_DIM:(hq % 2 + 1) * HEAD_DIM] = out.astype(o_ref.dtype)


def _attention(proj, kc, vc, cos, sin):
    rows = proj.shape[1]
    seq = rows // BATCH
    nq = seq // ATT_TQ
    c_start = np.arange(N_CMP_PAD) * CMP_STRIDE
    s_start = np.arange(N_SEL) * SEL_BLOCK
    ovt = ((c_start[None, :] < s_start[:, None] + SEL_BLOCK)
           & (c_start[None, :] + CMP_BLOCK > s_start[:, None])
           & (np.arange(N_CMP_PAD)[None, :] < N_CMP_PAD - 1)).astype(np.float32)
    expand = (np.arange(seq)[None, :] // SEL_BLOCK == np.arange(N_SEL)[:, None]).astype(np.float32)

    full = lambda sl: pl.BlockSpec((None, seq, SLAB), lambda b, qi: (sl, b, 0))
    cmp_spec = lambda: pl.BlockSpec((None, N_CMP_PAD, KV_W), lambda b, qi: (b, 0, 0))
    tab = lambda: pl.BlockSpec((seq, HEAD_DIM), lambda b, qi: (0, 0))
    nq3 = D_ATTN // SLAB
    return pl.pallas_call(
        _attn_kernel,
        grid=(BATCH, nq),
        in_specs=[pl.BlockSpec((nq3, ATT_TQ, SLAB), lambda b, qi: (SL_Q // nq3, b * nq + qi, 0)),
                  full(SL_KS), full(SL_VS), full(SL_KW), full(SL_VW),
                  cmp_spec(), cmp_spec(),
                  pl.BlockSpec((None, ATT_TQ, SLAB), lambda b, qi: (SL_CG, b * nq + qi, 0)),
                  tab(), tab(),
                  pl.BlockSpec((N_SEL, N_CMP_PAD), lambda b, qi: (0, 0)),
                  pl.BlockSpec((N_SEL, seq), lambda b, qi: (0, 0))],
        out_specs=pl.BlockSpec((nq3, ATT_TQ, SLAB), lambda b, qi: (0, b * nq + qi, 0)),
        out_shape=jax.ShapeDtypeStruct((nq3, rows, SLAB), BF16),
        scratch_shapes=[pltpu.VMEM((seq, KV_W), BF16),
                        pltpu.VMEM((seq, KV_W), BF16),
                        pltpu.VMEM((seq // ATT_TK, ATT_TQ, ATT_TK), F32)],
        compiler_params=_cparams(("parallel", "arbitrary")),
        name="sparse_attention",
    )(proj, proj, proj, proj, proj, kc, vc, proj, cos, sin, jnp.asarray(ovt),
      jnp.asarray(expand, BF16))


MERGE_TM = 256


def _cat_slabs(ref, n):
    return jnp.concatenate([ref[k] for k in range(n)], axis=-1)


def _merge_kernel(ua_ref, ub_ref, uc_ref, ga_ref, gb_ref, gc_ref, x_ref,
                  wa_ref, wb_ref, wc_ref, wo_ref, o_ref):
    ng = D_MODEL // SLAB
    y = None
    for u_ref, g_ref, w_ref in ((ua_ref, ga_ref, wa_ref), (ub_ref, gb_ref, wb_ref),
                                (uc_ref, gc_ref, wc_ref)):
        u = _cat_slabs(u_ref, u_ref.shape[0])
        p = jnp.dot(u, w_ref[...], preferred_element_type=F32)
        term = _sigmoid(_cat_slabs(g_ref, ng).astype(F32)) * p
        y = term if y is None else y + term
    o_ref[...] = x_ref[...] + jnp.dot(y.astype(BF16), wo_ref[...], preferred_element_type=F32)


def _merge(ua, ub, uc, proj, x2, wa, wb, wc, wo):
    rows, d = x2.shape
    ng = d // SLAB
    act = lambda n: pl.BlockSpec((n, MERGE_TM, SLAB), lambda i: (0, i, 0))
    gate = lambda sl: pl.BlockSpec((ng, MERGE_TM, SLAB), lambda i: (sl // ng, i, 0))
    res = lambda k: pl.BlockSpec((k, d), lambda i: (0, 0), pipeline_mode=pl.Buffered(1))
    return pl.pallas_call(
        _merge_kernel,
        grid=(rows // MERGE_TM,),
        in_specs=[act(ua.shape[0]), act(ub.shape[0]), act(uc.shape[0]),
                  gate(SL_GA), gate(SL_GB), gate(SL_GC),
                  pl.BlockSpec((MERGE_TM, d), lambda i: (i, 0)),
                  res(wa.shape[0]), res(wb.shape[0]), res(wc.shape[0]), res(wo.shape[0])],
        out_specs=pl.BlockSpec((MERGE_TM, d), lambda i: (i, 0)),
        out_shape=jax.ShapeDtypeStruct((rows, d), F32),
        compiler_params=_cparams(("parallel",)),
        name="merge_out_proj",
    )(ua, ub, uc, proj, proj, proj, x2, wa, wb, wc, wo)


MLP_TM = 512
MLP_TF = 512


def _mlp_kernel(x_ref, g_ref, wu_ref, wd_ref, o_ref, h_ref):
    @pl.when(pl.program_id(1) == 0)
    def _():
        _rmsnorm_to(h_ref, x_ref, g_ref, MLP_TM)
        o_ref[...] = x_ref[...]

    a = jnp.dot(h_ref[...], wu_ref[...], preferred_element_type=F32)
    a = jnp.maximum(a, 0.0)
    o_ref[...] += jnp.dot((a * a).astype(BF16), wd_ref[...], preferred_element_type=F32)


def _mlp(x2, g, wu, wd):
    rows, d = x2.shape
    f = wu.shape[1]
    return pl.pallas_call(
        _mlp_kernel,
        grid=(rows // MLP_TM, f // MLP_TF),
        in_specs=[pl.BlockSpec((MLP_TM, d), lambda i, j: (i, 0)),
                  pl.BlockSpec((1, d), lambda i, j: (0, 0)),
                  pl.BlockSpec((d, MLP_TF), lambda i, j: (0, j)),
                  pl.BlockSpec((MLP_TF, d), lambda i, j: (j, 0))],
        out_specs=pl.BlockSpec((MLP_TM, d), lambda i, j: (i, 0)),
        out_shape=jax.ShapeDtypeStruct((rows, d), F32),
        scratch_shapes=[pltpu.VMEM((MLP_TM, d), BF16)],
        compiler_params=_cparams(("parallel", "arbitrary")),
        name="mlp",
    )(x2, g.reshape(1, d), wu, wd)


FINAL_TM = 512


def _final_norm_kernel(x_ref, g_ref, o_ref):
    _rmsnorm_to(o_ref, x_ref, g_ref, FINAL_TM)


def _final_norm(x2, g):
    rows, d = x2.shape
    return pl.pallas_call(
        _final_norm_kernel,
        grid=(rows // FINAL_TM,),
        in_specs=[pl.BlockSpec((FINAL_TM, d), lambda i: (i, 0)),
                  pl.BlockSpec((1, d), lambda i: (0, 0))],
        out_specs=pl.BlockSpec((FINAL_TM, d), lambda i: (i, 0)),
        out_shape=jax.ShapeDtypeStruct((rows, d), x2.dtype),
        compiler_params=_cparams(("parallel",)),
        name="final_norm",
    )(x2, g.reshape(1, d))


def _permute_w_in(w):
    cols = lambda off, n: w[:, off:off + n]
    pad = jnp.zeros((w.shape[0], SLAB - 3 * N_Q_HEADS), w.dtype)
    parts = [cols(_O_GA, D_MODEL), cols(_O_GB, D_MODEL), cols(_O_GC, D_MODEL),
             cols(_O_RX, D_RNN), cols(_O_RG, D_RNN), cols(_O_Q, D_ATTN),
             cols(_O_KC, KV_W), cols(_O_VC, KV_W), cols(_O_KS, KV_W),
             cols(_O_AV, D_CONV), cols(_O_AG, D_CONV),
             cols(_O_VS, KV_W), cols(_O_KW, KV_W), cols(_O_VW, KV_W),
             cols(_O_CG, 3 * N_Q_HEADS), pad]
    return jnp.concatenate(parts, axis=1).astype(BF16)


def _rope_tables(s):
    inv = 1.0 / (ROPE_THETA ** (jnp.arange(0, HEAD_DIM, 2, dtype=F32) / HEAD_DIM))
    ang = jnp.arange(s, dtype=F32)[:, None] * inv[None, :]
    cos, sin = jnp.cos(ang), jnp.sin(ang)
    return jnp.concatenate([cos, cos], axis=-1), jnp.concatenate([-sin, sin], axis=-1)


def _layer(x2, cos, sin, attn_norm_g, w_in, conv_dw_w, conv_dw_b, conv_ln_g, conv_ln_b,
           w_conv_out, rnn_conv_w, rnn_conv_b, rglru_wa, rglru_ba, rglru_wx, rglru_bx,
           rglru_lambda, w_rnn_out, cmp_pe, cmp_k_w1, cmp_k_w2, cmp_v_w1, cmp_v_w2, w_attn_out,
           w_o, mlp_norm_g, w_mlp_up, w_mlp_down):
    proj = _in_projection(x2, attn_norm_g, _permute_w_in(w_in))
    ua = _conv_branch(proj, conv_dw_w, conv_dw_b, conv_ln_g, conv_ln_b)
    ub = _rglru_branch(proj, rnn_conv_w, rnn_conv_b, rglru_wa, rglru_ba, rglru_wx, rglru_bx,
                       rglru_lambda)
    kc, vc = _compress(proj, cmp_pe, cmp_k_w1, cmp_k_w2, cmp_v_w1, cmp_v_w2)
    uc = _attention(proj, kc, vc, cos, sin)
    x2 = _merge(ua, ub, uc, proj, x2, w_conv_out.astype(BF16), w_rnn_out.astype(BF16),
                w_attn_out.astype(BF16), w_o.astype(BF16))
    return _mlp(x2, mlp_norm_g, w_mlp_up.astype(BF16), w_mlp_down.astype(BF16))


def kernel(x, attn_norm_g, w_in, conv_dw_w, conv_dw_b, conv_ln_g, conv_ln_b, w_conv_out, rnn_conv_w, rnn_conv_b, rglru_wa, rglru_ba, rglru_wx, rglru_bx, rglru_lambda, w_rnn_out, cmp_pe, cmp_k_w1, cmp_k_w2, cmp_v_w1, cmp_v_w2, w_attn_out, w_o, mlp_norm_g, w_mlp_up, w_mlp_down, final_norm_g):
    b, s, d = x.shape
    assert (b, s, d) == (BATCH, SEQ, D_MODEL)
    cos, sin = _rope_tables(s)
    x2 = x.reshape(b * s, d)
    per_layer = (attn_norm_g, w_in, conv_dw_w, conv_dw_b, conv_ln_g, conv_ln_b, w_conv_out,
                 rnn_conv_w, rnn_conv_b, rglru_wa, rglru_ba, rglru_wx, rglru_bx, rglru_lambda,
                 w_rnn_out, cmp_pe, cmp_k_w1, cmp_k_w2, cmp_v_w1, cmp_v_w2, w_attn_out, w_o,
                 mlp_norm_g, w_mlp_up, w_mlp_down)
    for l in range(DEPTH):
        x2 = _layer(x2, cos, sin, *[p[l] for p in per_layer])
    return _final_norm(x2, final_norm_g).reshape(b, s, d)
```

```python
import numpy as np
import jax
import jax.numpy as jnp
from jax import lax
from jax.experimental import pallas as pl
from jax.experimental.pallas import tpu as pltpu

F32 = jnp.float32
BF16 = jnp.bfloat16

D_MODEL = 2048
BATCH = 8
SEQ = 2048
DEPTH = 2

D_CONV = D_MODEL // 4
CONV_WIDTH = 31
D_RNN = 3 * D_MODEL // 8
RNN_BLOCKS = 6
RNN_BLOCK_W = D_RNN // RNN_BLOCKS
RNN_CONV_WIDTH = 4
RG_C = 8.0
N_Q_HEADS = 6
N_KV_HEADS = 2
HEAD_DIM = 128
GROUP = N_Q_HEADS // N_KV_HEADS
D_ATTN = N_Q_HEADS * HEAD_DIM
KV_W = N_KV_HEADS * HEAD_DIM
CMP_BLOCK = 32
CMP_STRIDE = 16
SEL_BLOCK = 64
SEL_TOP_N = 16
WINDOW = 512
ROPE_THETA = 10000.0
D_FF = 4 * D_MODEL
NORM_EPS = 1e-6
NEG_INF = -1e30
POS_INF = 1e30

N_CMP_PAD = SEQ // CMP_STRIDE
N_SEL = SEQ // SEL_BLOCK

LANES = 128
SLAB = 256
SL_GA, SL_GB, SL_GC = 0, 8, 16
SL_RX, SL_RG = 24, 27
SL_Q = 30
SL_KC, SL_VC, SL_KS = 33, 34, 35
SL_AV, SL_AG = 36, 38
SL_VS, SL_KW, SL_VW, SL_CG = 40, 41, 42, 43
N_SLABS = 44
N_IN_PAD = N_SLABS * SLAB

_IN_SIZES = (D_CONV, D_CONV, D_RNN, D_RNN, D_ATTN, KV_W, KV_W, KV_W, KV_W, KV_W, KV_W,
             3 * N_Q_HEADS, D_MODEL, D_MODEL, D_MODEL)
_IN_OFF = np.concatenate([[0], np.cumsum(_IN_SIZES)])
(_O_AV, _O_AG, _O_RX, _O_RG, _O_Q, _O_KC, _O_VC, _O_KS, _O_VS, _O_KW, _O_VW, _O_CG,
 _O_GA, _O_GB, _O_GC) = [int(v) for v in _IN_OFF[:-1]]
N_IN = int(_IN_OFF[-1])

VMEM_LIMIT = 56 * 1024 * 1024


def _cparams(sem, vmem=VMEM_LIMIT):
    return pltpu.CompilerParams(dimension_semantics=sem, vmem_limit_bytes=vmem)


def _sigmoid(x):
    return 1.0 / (1.0 + jnp.exp(-x))


def _gelu_tanh(x):
    c = np.float32(np.sqrt(2.0 / np.pi))
    return 0.5 * x * (1.0 + jnp.tanh(c * (x + 0.044715 * (x * x * x))))


IN_TM = 1024
IN_TN = 1024
NORM_RC = 128


def _rmsnorm_to(h_ref, x_ref, g_ref, rows):
    g = g_ref[...]

    def body(c, carry):
        r0 = pl.multiple_of(c * NORM_RC, NORM_RC)
        x = x_ref[pl.ds(r0, NORM_RC), :]
        ms = jnp.mean(x * x, axis=-1, keepdims=True)
        h_ref[pl.ds(r0, NORM_RC), :] = (x * lax.rsqrt(ms + NORM_EPS) * g).astype(h_ref.dtype)
        return carry

    lax.fori_loop(0, rows // NORM_RC, body, 0)


def _inproj_kernel(x_ref, g_ref, w_ref, o_ref, h_ref):
    @pl.when(pl.program_id(1) == 0)
    def _():
        _rmsnorm_to(h_ref, x_ref, g_ref, IN_TM)

    for k in range(IN_TN // SLAB):
        r = jnp.dot(h_ref[...], w_ref[:, k * SLAB:(k + 1) * SLAB], preferred_element_type=F32)
        o_ref[k] = r.astype(o_ref.dtype)


def _in_projection(x2, g, w_perm):
    rows, d = x2.shape
    return pl.pallas_call(
        _inproj_kernel,
        grid=(rows // IN_TM, N_IN_PAD // IN_TN),
        in_specs=[pl.BlockSpec((IN_TM, d), lambda i, j: (i, 0)),
                  pl.BlockSpec((1, d), lambda i, j: (0, 0)),
                  pl.BlockSpec((d, IN_TN), lambda i, j: (0, j))],
        out_specs=pl.BlockSpec((IN_TN // SLAB, IN_TM, SLAB), lambda i, j: (j, i, 0)),
        out_shape=jax.ShapeDtypeStruct((N_SLABS, rows, SLAB), BF16),
        scratch_shapes=[pltpu.VMEM((IN_TM, d), BF16)],
        compiler_params=_cparams(("parallel", "arbitrary")),
        name="in_projection",
    )(x2, g.reshape(1, d), w_perm)


MIX_TS = 256
MIX_TR = MIX_TS * BATCH
CONV_HALO = 256
CONV_RC = 64


def _conv_kernel(v_ref, g_ref, w_ref, b_ref, lg_ref, lb_ref, o_ref, ubuf, ybuf):
    nc = D_CONV // LANES
    per_slab = SLAB // LANES

    @pl.when(pl.program_id(0) == 0)
    def _():
        ubuf[:, 0:CONV_HALO, :] = jnp.zeros((nc, CONV_HALO, LANES), F32)

    def glu(b, carry):
        for c in range(nc):
            k, ls = c // per_slab, slice((c % per_slab) * LANES, (c % per_slab + 1) * LANES)
            v = v_ref[k, b, :, ls].astype(F32)
            g = g_ref[k, b, :, ls].astype(F32)
            ubuf[c, pl.ds(CONV_HALO + b, MIX_TS, stride=BATCH), :] = v * _sigmoid(g)
        return carry

    lax.fori_loop(0, BATCH, glu, 0)

    base = CONV_HALO - (CONV_WIDTH - 1) * BATCH

    def conv(i, carry):
        r0 = pl.multiple_of(i * CONV_RC, CONV_RC)
        acc = []
        for c in range(nc):
            ls = slice(c * LANES, (c + 1) * LANES)
            a = jnp.zeros((CONV_RC, LANES), F32)
            for j in range(CONV_WIDTH):
                a = a + w_ref[j:j + 1, ls] * ubuf[c, pl.ds(r0 + base + BATCH * j, CONV_RC), :]
            acc.append(a + b_ref[:, ls])
        mu = sum(jnp.sum(a, axis=-1, keepdims=True) for a in acc) * (1.0 / D_CONV)
        cen = [a - mu for a in acc]
        var = sum(jnp.sum(a * a, axis=-1, keepdims=True) for a in cen) * (1.0 / D_CONV)
        inv = lax.rsqrt(var + NORM_EPS)
        for c in range(nc):
            ls = slice(c * LANES, (c + 1) * LANES)
            y = cen[c] * inv * lg_ref[:, ls] + lb_ref[:, ls]
            ybuf[c, pl.ds(r0, CONV_RC), :] = y * _sigmoid(y)
        return carry

    lax.fori_loop(0, MIX_TR // CONV_RC, conv, 0)

    def put(b, carry):
        for c in range(nc):
            k, ls = c // per_slab, slice((c % per_slab) * LANES, (c % per_slab + 1) * LANES)
            o_ref[k, b, :, ls] = ybuf[c, pl.ds(b, MIX_TS, stride=BATCH), :].astype(o_ref.dtype)
        return carry

    lax.fori_loop(0, BATCH, put, 0)
    ubuf[:, 0:CONV_HALO, :] = ubuf[:, MIX_TR:MIX_TR + CONV_HALO, :]


def _conv_branch(proj, w, b, lg, lb):
    seq = proj.shape[1] // BATCH
    p4 = proj.reshape(N_SLABS, BATCH, seq, SLAB)
    nk = D_CONV // SLAB
    vec = lambda: pl.BlockSpec((1, D_CONV), lambda i: (0, 0))
    out = pl.pallas_call(
        _conv_kernel,
        grid=(seq // MIX_TS,),
        in_specs=[pl.BlockSpec((nk, BATCH, MIX_TS, SLAB), lambda i: (SL_AV // nk, 0, i, 0)),
                  pl.BlockSpec((nk, BATCH, MIX_TS, SLAB), lambda i: (SL_AG // nk, 0, i, 0)),
                  pl.BlockSpec((CONV_WIDTH, D_CONV), lambda i: (0, 0)),
                  vec(), vec(), vec()],
        out_specs=pl.BlockSpec((nk, BATCH, MIX_TS, SLAB), lambda i: (0, 0, i, 0)),
        out_shape=jax.ShapeDtypeStruct((nk, BATCH, seq, SLAB), BF16),
        scratch_shapes=[pltpu.VMEM((D_CONV // LANES, CONV_HALO + MIX_TR, LANES), F32),
                        pltpu.VMEM((D_CONV // LANES, MIX_TR, LANES), F32)],
        compiler_params=_cparams(("arbitrary",)),
        name="conv_branch",
    )(p4, p4, w, b.reshape(1, -1), lg.reshape(1, -1), lb.reshape(1, -1))
    return out.reshape(nk, BATCH * seq, SLAB)


RNN_HALO = 32
RNN_RC = 256


def _rglru_kernel(x_ref, gate_ref, cw_ref, cb_ref, wa_ref, ba_ref, wx_ref, bx_ref, lam_ref,
                  o_ref, xbuf, abuf, gbuf, hstate):
    per_slab = SLAB // RNN_BLOCK_W

    @pl.when(pl.program_id(0) == 0)
    def _():
        xbuf[:, 0:RNN_HALO, :] = jnp.zeros((RNN_BLOCKS, RNN_HALO, RNN_BLOCK_W), F32)
        hstate[...] = jnp.zeros_like(hstate)

    def slab_cols(n):
        return n // per_slab, slice((n % per_slab) * RNN_BLOCK_W, (n % per_slab + 1) * RNN_BLOCK_W)

    def load(b, carry):
        for n in range(RNN_BLOCKS):
            k, ls = slab_cols(n)
            xbuf[n, pl.ds(RNN_HALO + b, MIX_TS, stride=BATCH), :] = x_ref[k, b, :, ls].astype(F32)
        return carry

    lax.fori_loop(0, BATCH, load, 0)

    z = -lam_ref[...]
    softplus = jnp.maximum(z, 0.0) + jnp.log(1.0 + jnp.exp(-jnp.abs(z)))
    coef = -RG_C * softplus
    base = RNN_HALO - (RNN_CONV_WIDTH - 1) * BATCH

    def gates(i, carry):
        r0 = pl.multiple_of(i * RNN_RC, RNN_RC)
        for n in range(RNN_BLOCKS):
            cs = slice(n * RNN_BLOCK_W, (n + 1) * RNN_BLOCK_W)
            y = jnp.zeros((RNN_RC, RNN_BLOCK_W), F32)
            for j in range(RNN_CONV_WIDTH):
                y = y + cw_ref[j:j + 1, cs] * xbuf[n, pl.ds(r0 + base + BATCH * j, RNN_RC), :]
            y = y + cb_ref[:, cs]
            yb = y.astype(BF16)
            ra = _sigmoid(jnp.dot(yb, wa_ref[n], preferred_element_type=F32) + ba_ref[:, cs])
            ri = _sigmoid(jnp.dot(yb, wx_ref[n], preferred_element_type=F32) + bx_ref[:, cs])
            a = jnp.exp(coef[:, cs] * ra)
            abuf[n, pl.ds(r0, RNN_RC), :] = a
            gbuf[n, pl.ds(r0, RNN_RC), :] = jnp.sqrt(1.0 - a * a) * (ri * y)
        return carry

    lax.fori_loop(0, MIX_TR // RNN_RC, gates, 0)

    def step(t, h):
        r0 = pl.multiple_of(t * BATCH, BATCH)
        h = abuf[:, pl.ds(r0, BATCH), :] * h + gbuf[:, pl.ds(r0, BATCH), :]
        gbuf[:, pl.ds(r0, BATCH), :] = h
        return h

    hstate[...] = lax.fori_loop(0, MIX_TS, step, hstate[...], unroll=8)

    def put(b, carry):
        for n in range(RNN_BLOCKS):
            k, ls = slab_cols(n)
            h = gbuf[n, pl.ds(b, MIX_TS, stride=BATCH), :]
            o_ref[k, b, :, ls] = (h * _gelu_tanh(gate_ref[k, b, :, ls].astype(F32))).astype(o_ref.dtype)
        return carry

    lax.fori_loop(0, BATCH, put, 0)
    xbuf[:, 0:RNN_HALO, :] = xbuf[:, MIX_TR:MIX_TR + RNN_HALO, :]


def _rglru_branch(proj, cw, cb, wa, ba, wx, bx, lam):
    seq = proj.shape[1] // BATCH
    p4 = proj.reshape(N_SLABS, BATCH, seq, SLAB)
    nk = D_RNN // SLAB
    vec = lambda: pl.BlockSpec((1, D_RNN), lambda i: (0, 0))
    blk = lambda: pl.BlockSpec((RNN_BLOCKS, RNN_BLOCK_W, RNN_BLOCK_W), lambda i: (0, 0, 0))
    out = pl.pallas_call(
        _rglru_kernel,
        grid=(seq // MIX_TS,),
        in_specs=[pl.BlockSpec((nk, BATCH, MIX_TS, SLAB), lambda i: (SL_RX // nk, 0, i, 0)),
                  pl.BlockSpec((nk, BATCH, MIX_TS, SLAB), lambda i: (SL_RG // nk, 0, i, 0)),
                  pl.BlockSpec((RNN_CONV_WIDTH, D_RNN), lambda i: (0, 0)),
                  vec(), blk(), vec(), blk(), vec(), vec()],
        out_specs=pl.BlockSpec((nk, BATCH, MIX_TS, SLAB), lambda i: (0, 0, i, 0)),
        out_shape=jax.ShapeDtypeStruct((nk, BATCH, seq, SLAB), BF16),
        scratch_shapes=[pltpu.VMEM((RNN_BLOCKS, RNN_HALO + MIX_TR, RNN_BLOCK_W), F32),
                        pltpu.VMEM((RNN_BLOCKS, MIX_TR, RNN_BLOCK_W), F32),
                        pltpu.VMEM((RNN_BLOCKS, MIX_TR, RNN_BLOCK_W), F32),
                        pltpu.VMEM((RNN_BLOCKS, BATCH, RNN_BLOCK_W), F32)],
        compiler_params=_cparams(("arbitrary",)),
        name="rglru_branch",
    )(p4, p4, cw, cb.reshape(1, -1), wa.astype(BF16), ba.reshape(1, -1),
      wx.astype(BF16), bx.reshape(1, -1), lam.reshape(1, -1))
    return out.reshape(nk, BATCH * seq, SLAB)


def _compress_kernel(k_ref, v_ref, pe_ref, kw1_ref, vw1_ref, kw2_ref, vw2_ref, kc_ref, vc_ref,
                     stage, acc, pacc):
    for src, w1_ref, w2_ref, dst in ((k_ref, kw1_ref, kw2_ref, kc_ref),
                                     (v_ref, vw1_ref, vw2_ref, vc_ref)):
        for h in range(N_KV_HEADS):
            stage[h] = src[:, h * HEAD_DIM:(h + 1) * HEAD_DIM].astype(F32)
        acc[...] = jnp.zeros_like(acc)
        pacc[...] = jnp.zeros_like(pacc)

        def body(l, carry):
            w1 = w1_ref[l]
            for h in range(N_KV_HEADS):
                x = stage[h, pl.ds(l, N_CMP_PAD, stride=CMP_STRIDE), :]
                acc[h] += jnp.dot(x.astype(BF16), w1, preferred_element_type=F32)
            pe = pe_ref[l].astype(BF16)
            pacc[...] += (jnp.dot(pe[:, :HEAD_DIM], w1[:, :HEAD_DIM], preferred_element_type=F32)
                          + jnp.dot(pe[:, HEAD_DIM:], w1[:, HEAD_DIM:], preferred_element_type=F32))
            return carry

        lax.fori_loop(0, CMP_STRIDE, body, 0)

        for h in range(N_KV_HEADS):
            p = acc[h]
            hi_next = pltpu.roll(p[:, HEAD_DIM:], N_CMP_PAD - 1, 0)
            pre = p[:, :HEAD_DIM] + hi_next + pacc[0:1, :]
            y = jnp.dot(_gelu_tanh(pre).astype(BF16), w2_ref[...], preferred_element_type=F32)
            dst[:, h * HEAD_DIM:(h + 1) * HEAD_DIM] = y.astype(dst.dtype)


def _compress(proj, pe, kw1, kw2, vw1, vw2):
    seq = proj.shape[1] // BATCH
    pe2 = jnp.zeros((CMP_STRIDE, 8, 2 * HEAD_DIM), F32)
    pe2 = pe2.at[:, 0, :HEAD_DIM].set(pe[:CMP_STRIDE]).at[:, 0, HEAD_DIM:].set(pe[CMP_STRIDE:])

    def w1cat(w1):
        w = w1.reshape(2, CMP_STRIDE, HEAD_DIM, HEAD_DIM)
        return jnp.concatenate([w[0], w[1]], axis=-1).astype(BF16)

    kv_spec = lambda sl: pl.BlockSpec((None, seq, SLAB), lambda b: (sl, b, 0))
    w1_spec = lambda: pl.BlockSpec((CMP_STRIDE, HEAD_DIM, 2 * HEAD_DIM), lambda b: (0, 0, 0))
    w2_spec = lambda: pl.BlockSpec((HEAD_DIM, HEAD_DIM), lambda b: (0, 0))
    out_spec = lambda: pl.BlockSpec((None, N_CMP_PAD, KV_W), lambda b: (b, 0, 0))
    return pl.pallas_call(
        _compress_kernel,
        grid=(BATCH,),
        in_specs=[kv_spec(SL_KC), kv_spec(SL_VC),
                  pl.BlockSpec((CMP_STRIDE, 8, 2 * HEAD_DIM), lambda b: (0, 0, 0)),
                  w1_spec(), w1_spec(), w2_spec(), w2_spec()],
        out_specs=[out_spec(), out_spec()],
        out_shape=[jax.ShapeDtypeStruct((BATCH, N_CMP_PAD, KV_W), BF16)] * 2,
        scratch_shapes=[pltpu.VMEM((N_KV_HEADS, seq, HEAD_DIM), F32),
                        pltpu.VMEM((N_KV_HEADS, N_CMP_PAD, 2 * HEAD_DIM), F32),
                        pltpu.VMEM((8, HEAD_DIM), F32)],
        compiler_params=_cparams(("parallel",)),
        name="compress_kv",
    )(proj, proj, pe2, w1cat(kw1), w1cat(vw1), kw2.astype(BF16), vw2.astype(BF16))


ATT_TQ = 256
ATT_TK = 256
ATT_NKB = SEQ // ATT_TK
ATT_WB = WINDOW // ATT_TK + 1
SEL_PER_TK = ATT_TK // SEL_BLOCK
ATT_LANES = GROUP * ATT_TQ


def _tile_heads(x):
    return jnp.concatenate([x] * GROUP, axis=1)


def _attn_kernel(q_ref, ks_ref, vs_ref, kw_ref, vw_ref, kc_ref, vc_ref, cg_ref, cos_ref, sin_ref,
                 cost_ref, sint_ref, ovt_ref, o_ref, ksr, kwr, vst, vwt, vct, selb):
    qi = pl.program_id(1)
    scale = np.float32(HEAD_DIM ** -0.5)
    half = HEAD_DIM // 2

    @pl.when(qi == 0)
    def _():
        def prep(c, carry):
            r0 = pl.multiple_of(c * ATT_TK, ATT_TK)
            cos = cos_ref[pl.ds(r0, ATT_TK), :]
            sin = sin_ref[pl.ds(r0, ATT_TK), :]
            for h in range(N_KV_HEADS):
                cs = slice(h * HEAD_DIM, (h + 1) * HEAD_DIM)
                for src, dst in ((ks_ref, ksr), (kw_ref, kwr)):
                    x = src[pl.ds(r0, ATT_TK), cs].astype(F32)
                    dst[pl.ds(r0, ATT_TK), cs] = (
                        x * cos + pltpu.roll(x, half, 1) * sin).astype(dst.dtype)
                for src, dst in ((vs_ref, vst), (vw_ref, vwt)):
                    dst[h, c] = src[pl.ds(r0, ATT_TK), cs].astype(F32).T.astype(dst.dtype)
            return carry

        lax.fori_loop(0, ATT_NKB, prep, 0)
        for h in range(N_KV_HEADS):
            vct[h] = vc_ref[:, h * HEAD_DIM:(h + 1) * HEAD_DIM].astype(F32).T.astype(vct.dtype)

    t0 = qi * ATT_TQ
    cos_t = cost_ref[...]
    sin_t = sint_ref[...]
    gate_t = _sigmoid(cg_ref[:, 0:HEAD_DIM].astype(F32).T)

    t_c = t0 + (lax.broadcasted_iota(jnp.int32, (N_CMP_PAD, ATT_LANES), 1) & (ATT_TQ - 1))
    n_sub = lax.broadcasted_iota(jnp.int32, (N_CMP_PAD, ATT_LANES), 0)
    valid_c = (n_sub * CMP_STRIDE + (CMP_BLOCK - 1)) <= t_c
    t_s = t0 + lax.broadcasted_iota(jnp.int32, (N_SEL, ATT_TQ), 1)
    m_sub = lax.broadcasted_iota(jnp.int32, (N_SEL, ATT_TQ), 0)
    cur = t_s >> 6
    valid_s = m_sub <= cur
    forced = (m_sub == 0) | (m_sub == cur) | (m_sub == cur - 1)
    key_io = lax.broadcasted_iota(jnp.int32, (ATT_TK, ATT_TQ), 0)
    qry_io = lax.broadcasted_iota(jnp.int32, (ATT_TK, ATT_TQ), 1)
    diag_bias = jnp.where(key_io <= qry_io, 0.0, NEG_INF)
    w_blk = jnp.maximum(qi - WINDOW // ATT_TK, 0)
    w_start = pl.multiple_of(w_blk * ATT_TK, ATT_TK)
    dist = ((t0 + lax.broadcasted_iota(jnp.int32, (ATT_WB * ATT_TK, ATT_TQ), 1))
            - (w_start + lax.broadcasted_iota(jnp.int32, (ATT_WB * ATT_TK, ATT_TQ), 0)))
    win_bias = jnp.where((dist >= 0) & (dist < WINDOW), 0.0, NEG_INF)

    for h in range(N_KV_HEADS):
        hs = slice(h * HEAD_DIM, (h + 1) * HEAD_DIM)
        q_plain, q_rot = [], []
        for g in range(GROUP):
            hq = h * GROUP + g
            xt = q_ref[hq // 2, :, (hq % 2) * HEAD_DIM:(hq % 2 + 1) * HEAD_DIM].astype(F32).T
            swapped = jnp.concatenate([xt[half:], xt[:half]], axis=0)
            q_plain.append((xt * scale).astype(BF16))
            q_rot.append(((xt * cos_t + swapped * sin_t) * scale).astype(BF16))
        qn3 = jnp.concatenate(q_plain, axis=1)
        qr3 = jnp.concatenate(q_rot, axis=1)

        s = jnp.where(valid_c, jnp.dot(kc_ref[:, hs], qn3, preferred_element_type=F32), NEG_INF)
        mx = jnp.max(s, axis=0, keepdims=True)
        e = jnp.where(valid_c, jnp.exp(s - mx), 0.0)
        den = jnp.sum(e, axis=0, keepdims=True)
        p = e / jnp.where(den > 0.0, den, 1.0)
        o_cmp = jnp.dot(vct[h], p.astype(BF16), preferred_element_type=F32)
        p_sum = p[:, 0:ATT_TQ]
        for g in range(1, GROUP):
            p_sum = p_sum + p[:, g * ATT_TQ:(g + 1) * ATT_TQ]

        imp = jnp.dot(ovt_ref[...], p_sum, preferred_element_type=F32,
                      precision=lax.Precision.HIGHEST)
        score = jnp.where(valid_s, jnp.where(forced, POS_INF, imp), NEG_INF)
        rank = jnp.zeros((N_SEL, ATT_TQ), F32)
        for mp in range(N_SEL):
            row = score[mp:mp + 1, :]
            beats = (row > score) | ((row == score) & (m_sub > mp))
            rank = rank + jnp.where(beats, 1.0, 0.0)
        selb[...] = jnp.where((rank < SEL_TOP_N) & valid_s, 0.0, NEG_INF)

        def sel_block(j, carry, extra_bias):
            m_run, l_run, acc = carry
            k0 = pl.multiple_of(j * ATT_TK, ATT_TK)
            rows = [jnp.broadcast_to(selb[pl.ds(j * SEL_PER_TK + i, 1), :], (SEL_BLOCK, ATT_TQ))
                    for i in range(SEL_PER_TK)]
            bias = jnp.concatenate(rows, axis=0)
            if extra_bias is not None:
                bias = bias + extra_bias
            sc = (jnp.dot(ksr[pl.ds(k0, ATT_TK), hs], qr3, preferred_element_type=F32)
                  + _tile_heads(bias))
            m_new = jnp.maximum(m_run, jnp.max(sc, axis=0, keepdims=True))
            alpha = jnp.exp(m_run - m_new)
            pr = jnp.exp(sc - m_new)
            l_new = alpha * l_run + jnp.sum(pr, axis=0, keepdims=True)
            acc = alpha * acc + jnp.dot(vst[h, j], pr.astype(BF16), preferred_element_type=F32)
            return m_new, l_new, acc

        init = (jnp.full((1, ATT_LANES), NEG_INF, F32), jnp.zeros((1, ATT_LANES), F32),
                jnp.zeros((HEAD_DIM, ATT_LANES), F32))
        carry = lax.fori_loop(0, qi, lambda j, c: sel_block(j, c, None), init)
        _, l_s, acc_s = sel_block(qi, carry, diag_bias)
        o_sel = acc_s / l_s

        sw = (jnp.dot(kwr[pl.ds(w_start, ATT_WB * ATT_TK), hs], qr3, preferred_element_type=F32)
              + _tile_heads(win_bias))
        mw = jnp.max(sw, axis=0, keepdims=True)
        pw = jnp.exp(sw - mw)
        den_w = jnp.sum(pw, axis=0, keepdims=True)
        pw = pw.astype(BF16)
        o_win = None
        for i in range(ATT_WB):
            part = jnp.dot(vwt[h, w_blk + i], pw[i * ATT_TK:(i + 1) * ATT_TK],
                           preferred_element_type=F32)
            o_win = part if o_win is None else o_win + part
        o_win = o_win / den_w

        for g in range(GROUP):
            hq = h * GROUP + g
            ls = slice(g * ATT_TQ, (g + 1) * ATT_TQ)
            out_t = (gate_t[3 * hq:3 * hq + 1, :] * o_cmp[:, ls]
                     + gate_t[3 * hq + 1:3 * hq + 2, :] * o_sel[:, ls]
                     + gate_t[3 * hq + 2:3 * hq + 3, :] * o_win[:, ls])
            o_ref[hq // 2, :, (hq % 2) * HEAD_DIM:(hq % 2 + 1) * HEAD_DIM] = out_t.T.astype(o_ref.dtype)


def _attention(proj, kc, vc, tables):
    cos, sin, cos_t, sin_t = tables
    rows = proj.shape[1]
    seq = rows // BATCH
    nq = seq // ATT_TQ
    c_start = np.arange(N_CMP_PAD) * CMP_STRIDE
    s_start = np.arange(N_SEL) * SEL_BLOCK
    ovt = ((c_start[None, :] < s_start[:, None] + SEL_BLOCK)
           & (c_start[None, :] + CMP_BLOCK > s_start[:, None])
           & (np.arange(N_CMP_PAD)[None, :] < N_CMP_PAD - 1)).astype(np.float32)

    full = lambda sl: pl.BlockSpec((None, seq, SLAB), lambda b, qi: (sl, b, 0))
    cmp_spec = lambda: pl.BlockSpec((None, N_CMP_PAD, KV_W), lambda b, qi: (b, 0, 0))
    tab = lambda: pl.BlockSpec((seq, HEAD_DIM), lambda b, qi: (0, 0))
    tab_t = lambda: pl.BlockSpec((None, HEAD_DIM, ATT_TQ), lambda b, qi: (qi, 0, 0))
    nq3 = D_ATTN // SLAB
    kv_t = lambda: pltpu.VMEM((N_KV_HEADS, ATT_NKB, HEAD_DIM, ATT_TK), BF16)
    return pl.pallas_call(
        _attn_kernel,
        grid=(BATCH, nq),
        in_specs=[pl.BlockSpec((nq3, ATT_TQ, SLAB), lambda b, qi: (SL_Q // nq3, b * nq + qi, 0)),
                  full(SL_KS), full(SL_VS), full(SL_KW), full(SL_VW),
                  cmp_spec(), cmp_spec(),
                  pl.BlockSpec((None, ATT_TQ, SLAB), lambda b, qi: (SL_CG, b * nq + qi, 0)),
                  tab(), tab(), tab_t(), tab_t(),
                  pl.BlockSpec((N_SEL, N_CMP_PAD), lambda b, qi: (0, 0))],
        out_specs=pl.BlockSpec((nq3, ATT_TQ, SLAB), lambda b, qi: (0, b * nq + qi, 0)),
        out_shape=jax.ShapeDtypeStruct((nq3, rows, SLAB), BF16),
        scratch_shapes=[pltpu.VMEM((seq, KV_W), BF16),
                        pltpu.VMEM((seq, KV_W), BF16),
                        kv_t(), kv_t(),
                        pltpu.VMEM((N_KV_HEADS, HEAD_DIM, N_CMP_PAD), BF16),
                        pltpu.VMEM((N_SEL, ATT_TQ), F32)],
        compiler_params=_cparams(("parallel", "arbitrary")),
        name="sparse_attention",
    )(proj, proj, proj, proj, proj, kc, vc, proj, cos, sin, cos_t, sin_t, jnp.asarray(ovt))


MERGE_TM = 512
MERGE_TN = 512


def _cat_slabs(ref, first, n):
    return jnp.concatenate([ref[first + k] for k in range(n)], axis=-1)


def _merge_kernel(ua_ref, ub_ref, uc_ref, ga_ref, gb_ref, gc_ref, x_ref,
                  wa_ref, wb_ref, wc_ref, wo_ref, o_ref, y_ref):
    per = MERGE_TN // SLAB
    branches = ((ua_ref, ga_ref, wa_ref), (ub_ref, gb_ref, wb_ref), (uc_ref, gc_ref, wc_ref))
    acts = [_cat_slabs(u_ref, 0, u_ref.shape[0]) for u_ref, _, _ in branches]
    for c in range(D_MODEL // MERGE_TN):
        cs = slice(c * MERGE_TN, (c + 1) * MERGE_TN)
        y = None
        for u, (_, g_ref, w_ref) in zip(acts, branches):
            p = jnp.dot(u, w_ref[:, cs], preferred_element_type=F32)
            term = _sigmoid(_cat_slabs(g_ref, c * per, per).astype(F32)) * p
            y = term if y is None else y + term
        y_ref[:, cs] = y.astype(y_ref.dtype)
    o_ref[...] = x_ref[...] + jnp.dot(y_ref[...], wo_ref[...], preferred_element_type=F32)


def _merge(ua, ub, uc, proj, x2, wa, wb, wc, wo):
    rows, d = x2.shape
    ng = d // SLAB
    act = lambda n: pl.BlockSpec((n, MERGE_TM, SLAB), lambda i: (0, i, 0))
    gate = lambda sl: pl.BlockSpec((ng, MERGE_TM, SLAB), lambda i: (sl // ng, i, 0))
    res = lambda k: pl.BlockSpec((k, d), lambda i: (0, 0), pipeline_mode=pl.Buffered(1))
    return pl.pallas_call(
        _merge_kernel,
        grid=(rows // MERGE_TM,),
        in_specs=[act(ua.shape[0]), act(ub.shape[0]), act(uc.shape[0]),
                  gate(SL_GA), gate(SL_GB), gate(SL_GC),
                  pl.BlockSpec((MERGE_TM, d), lambda i: (i, 0)),
                  res(wa.shape[0]), res(wb.shape[0]), res(wc.shape[0]), res(wo.shape[0])],
        out_specs=pl.BlockSpec((MERGE_TM, d), lambda i: (i, 0)),
        out_shape=jax.ShapeDtypeStruct((rows, d), F32),
        scratch_shapes=[pltpu.VMEM((MERGE_TM, d), BF16)],
        compiler_params=_cparams(("parallel",)),
        name="merge_out_proj",
    )(ua, ub, uc, proj, proj, proj, x2, wa, wb, wc, wo)


MLP_TM = 1024
MLP_TF = 512


def _mlp_kernel(x_ref, g_ref, wu_ref, wd_ref, o_ref, h_ref):
    @pl.when(pl.program_id(1) == 0)
    def _():
        _rmsnorm_to(h_ref, x_ref, g_ref, MLP_TM)
        o_ref[...] = x_ref[...]

    a = jnp.dot(h_ref[...], wu_ref[...], preferred_element_type=F32)
    a = jnp.maximum(a, 0.0)
    o_ref[...] += jnp.dot((a * a).astype(BF16), wd_ref[...], preferred_element_type=F32)


def _mlp(x2, g, wu, wd):
    rows, d = x2.shape
    f = wu.shape[1]
    return pl.pallas_call(
        _mlp_kernel,
        grid=(rows // MLP_TM, f // MLP_TF),
        in_specs=[pl.BlockSpec((MLP_TM, d), lambda i, j: (i, 0)),
                  pl.BlockSpec((1, d), lambda i, j: (0, 0)),
                  pl.BlockSpec((d, MLP_TF), lambda i, j: (0, j)),
                  pl.BlockSpec((MLP_TF, d), lambda i, j: (j, 0))],
        out_specs=pl.BlockSpec((MLP_TM, d), lambda i, j: (i, 0)),
        out_shape=jax.ShapeDtypeStruct((rows, d), F32),
        scratch_shapes=[pltpu.VMEM((MLP_TM, d), BF16)],
        compiler_params=_cparams(("parallel", "arbitrary")),
        name="mlp",
    )(x2, g.reshape(1, d), wu, wd)


FINAL_TM = 512


def _final_norm_kernel(x_ref, g_ref, o_ref):
    _rmsnorm_to(o_ref, x_ref, g_ref, FINAL_TM)


def _final_norm(x2, g):
    rows, d = x2.shape
    return pl.pallas_call(
        _final_norm_kernel,
        grid=(rows // FINAL_TM,),
        in_specs=[pl.BlockSpec((FINAL_TM, d), lambda i: (i, 0)),
                  pl.BlockSpec((1, d), lambda i: (0, 0))],
        out_specs=pl.BlockSpec((FINAL_TM, d), lambda i: (i, 0)),
        out_shape=jax.ShapeDtypeStruct((rows, d), x2.dtype),
        compiler_params=_cparams(("parallel",)),
        name="final_norm",
    )(x2, g.reshape(1, d))


PREP_TR = 256
PREP_RC = 16
_ALIGNED_MOVES = ((SL_RX * SLAB, _O_RX, 2 * D_RNN),
                  (SL_Q * SLAB, _O_Q, D_ATTN + 3 * KV_W),
                  (SL_AV * SLAB, _O_AV, 2 * D_CONV),
                  (SL_VS * SLAB, _O_VS, 3 * KV_W))


def _prep_w_in_kernel(w_ref, o_ref):
    n_gate = 3 * N_Q_HEADS
    shift = _O_GA - _O_CG

    def body(i, carry):
        rows = pl.ds(pl.multiple_of(i * PREP_RC, PREP_RC), PREP_RC)
        for dst, src, n in _ALIGNED_MOVES:
            o_ref[rows, dst:dst + n] = w_ref[rows, src:src + n].astype(o_ref.dtype)
        tail = w_ref[rows, _O_CG:N_IN]
        o_ref[rows, 0:3 * D_MODEL] = tail[:, shift:shift + 3 * D_MODEL].astype(o_ref.dtype)
        lane = lax.broadcasted_iota(jnp.int32, (PREP_RC, SLAB), 1)
        o_ref[rows, SL_CG * SLAB:(SL_CG + 1) * SLAB] = jnp.where(
            lane < n_gate, tail[:, 0:SLAB], 0.0).astype(o_ref.dtype)
        return carry

    lax.fori_loop(0, PREP_TR // PREP_RC, body, 0)


def _permute_w_in(w_all, layer):
    _, d, n_in = w_all.shape
    return pl.pallas_call(
        _prep_w_in_kernel,
        grid=(d // PREP_TR,),
        in_specs=[pl.BlockSpec((None, PREP_TR, n_in), lambda i: (layer, i, 0))],
        out_specs=pl.BlockSpec((PREP_TR, N_IN_PAD), lambda i: (i, 0)),
        out_shape=jax.ShapeDtypeStruct((d, N_IN_PAD), BF16),
        compiler_params=_cparams(("parallel",)),
        name="prep_w_in",
    )(w_all)


CAST_TR = 256


def _cast_kernel(w_ref, o_ref):
    o_ref[...] = w_ref[...].astype(o_ref.dtype)


def _cast_bf16(w_all, layer):
    _, r, c = w_all.shape
    tr = min(CAST_TR, r)
    return pl.pallas_call(
        _cast_kernel,
        grid=(r // tr,),
        in_specs=[pl.BlockSpec((None, tr, c), lambda i: (layer, i, 0))],
        out_specs=pl.BlockSpec((tr, c), lambda i: (i, 0)),
        out_shape=jax.ShapeDtypeStruct((r, c), BF16),
        compiler_params=_cparams(("parallel",)),
        name="cast_bf16",
    )(w_all)


def _rope_tables(s):
    inv = 1.0 / (ROPE_THETA ** (jnp.arange(0, HEAD_DIM, 2, dtype=F32) / HEAD_DIM))
    ang = jnp.arange(s, dtype=F32)[:, None] * inv[None, :]
    cos, sin = jnp.cos(ang), jnp.sin(ang)
    cos_f = jnp.concatenate([cos, cos], axis=-1)
    sin_f = jnp.concatenate([-sin, sin], axis=-1)
    tiles = lambda a: a.reshape(s // ATT_TQ, ATT_TQ, HEAD_DIM).transpose(0, 2, 1)
    return cos_f, sin_f, tiles(cos_f), tiles(sin_f)


def _layer(x2, tables, w_in_bf, w_mats_bf, attn_norm_g, conv_dw_w, conv_dw_b, conv_ln_g, conv_ln_b,
           rnn_conv_w, rnn_conv_b, rglru_wa, rglru_ba, rglru_wx, rglru_bx,
           rglru_lambda, cmp_pe, cmp_k_w1, cmp_k_w2, cmp_v_w1, cmp_v_w2, mlp_norm_g):
    w_conv_out, w_rnn_out, w_attn_out, w_o, w_mlp_up, w_mlp_down = w_mats_bf
    proj = _in_projection(x2, attn_norm_g, w_in_bf)
    ua = _conv_branch(proj, conv_dw_w, conv_dw_b, conv_ln_g, conv_ln_b)
    ub = _rglru_branch(proj, rnn_conv_w, rnn_conv_b, rglru_wa, rglru_ba, rglru_wx, rglru_bx,
                       rglru_lambda)
    kc, vc = _compress(proj, cmp_pe, cmp_k_w1, cmp_k_w2, cmp_v_w1, cmp_v_w2)
    uc = _attention(proj, kc, vc, tables)
    x2 = _merge(ua, ub, uc, proj, x2, w_conv_out, w_rnn_out, w_attn_out, w_o)
    return _mlp(x2, mlp_norm_g, w_mlp_up, w_mlp_down)


def kernel(x, attn_norm_g, w_in, conv_dw_w, conv_dw_b, conv_ln_g, conv_ln_b, w_conv_out, rnn_conv_w, rnn_conv_b, rglru_wa, rglru_ba, rglru_wx, rglru_bx, rglru_lambda, w_rnn_out, cmp_pe, cmp_k_w1, cmp_k_w2, cmp_v_w1, cmp_v_w2, w_attn_out, w_o, mlp_norm_g, w_mlp_up, w_mlp_down, final_norm_g):
    b, s, d = x.shape
    assert (b, s, d) == (BATCH, SEQ, D_MODEL)
    tables = _rope_tables(s)
    x2 = x.reshape(b * s, d)
    per_layer = (attn_norm_g, conv_dw_w, conv_dw_b, conv_ln_g, conv_ln_b,
                 rnn_conv_w, rnn_conv_b, rglru_wa, rglru_ba, rglru_wx, rglru_bx, rglru_lambda,
                 cmp_pe, cmp_k_w1, cmp_k_w2, cmp_v_w1, cmp_v_w2, mlp_norm_g)
    mats = (w_conv_out, w_rnn_out, w_attn_out, w_o, w_mlp_up, w_mlp_down)
    for l in range(DEPTH):
        x2 = _layer(x2, tables, _permute_w_in(w_in, l), [_cast_bf16(m, l) for m in mats],
                    *[p[l] for p in per_layer])
    return _final_norm(x2, final_norm_g).reshape(b, s, d)
```

```python
import functools

import numpy as np
import jax
import jax.numpy as jnp
from jax import lax
from jax.experimental import pallas as pl
from jax.experimental.pallas import tpu as pltpu

F32 = jnp.float32
BF16 = jnp.bfloat16

D_MODEL = 2048
BATCH = 8
SEQ = 2048
DEPTH = 2

D_CONV = D_MODEL // 4
CONV_WIDTH = 31
D_RNN = 3 * D_MODEL // 8
RNN_BLOCKS = 6
RNN_BLOCK_W = D_RNN // RNN_BLOCKS
RNN_CONV_WIDTH = 4
RG_C = 8.0
N_Q_HEADS = 6
N_KV_HEADS = 2
HEAD_DIM = 128
GROUP = N_Q_HEADS // N_KV_HEADS
D_ATTN = N_Q_HEADS * HEAD_DIM
KV_W = N_KV_HEADS * HEAD_DIM
CMP_BLOCK = 32
CMP_STRIDE = 16
SEL_BLOCK = 64
SEL_TOP_N = 16
WINDOW = 512
ROPE_THETA = 10000.0
D_FF = 4 * D_MODEL
NORM_EPS = 1e-6
NEG_INF = -1e30
POS_INF = 1e30

N_CMP_PAD = SEQ // CMP_STRIDE
N_SEL = SEQ // SEL_BLOCK

LANES = 128
SLAB = 256
SL_GA, SL_GB, SL_GC = 0, 8, 16
SL_RX, SL_RG = 24, 27
SL_Q = 30
SL_KC, SL_VC, SL_KS = 33, 34, 35
SL_AV, SL_AG = 36, 38
SL_VS, SL_KW, SL_VW, SL_CG = 40, 41, 42, 43
N_SLABS = 44
N_IN_PAD = N_SLABS * SLAB

_IN_SIZES = (D_CONV, D_CONV, D_RNN, D_RNN, D_ATTN, KV_W, KV_W, KV_W, KV_W, KV_W, KV_W,
             3 * N_Q_HEADS, D_MODEL, D_MODEL, D_MODEL)
_IN_OFF = np.concatenate([[0], np.cumsum(_IN_SIZES)])
(_O_AV, _O_AG, _O_RX, _O_RG, _O_Q, _O_KC, _O_VC, _O_KS, _O_VS, _O_KW, _O_VW, _O_CG,
 _O_GA, _O_GB, _O_GC) = [int(v) for v in _IN_OFF[:-1]]
N_IN = int(_IN_OFF[-1])

VMEM_LIMIT = 56 * 1024 * 1024


def _cparams(sem, vmem=VMEM_LIMIT):
    return pltpu.CompilerParams(dimension_semantics=sem, vmem_limit_bytes=vmem)


def _sigmoid(x):
    return 1.0 / (1.0 + jnp.exp(-x))


def _gelu_tanh(x):
    c = np.float32(np.sqrt(2.0 / np.pi))
    return 0.5 * x * (1.0 + jnp.tanh(c * (x + 0.044715 * (x * x * x))))


IN_TM = 1024
IN_TN = 1024
NORM_RC = 128


def _rmsnorm_to(h_ref, x_ref, g_ref, rows):
    g = g_ref[...]

    def body(c, carry):
        r0 = pl.multiple_of(c * NORM_RC, NORM_RC)
        x = x_ref[pl.ds(r0, NORM_RC), :]
        ms = jnp.mean(x * x, axis=-1, keepdims=True)
        h_ref[pl.ds(r0, NORM_RC), :] = (x * lax.rsqrt(ms + NORM_EPS) * g).astype(h_ref.dtype)
        return carry

    lax.fori_loop(0, rows // NORM_RC, body, 0)


def _inproj_kernel(x_ref, g_ref, w_ref, o_ref, h_ref):
    @pl.when(pl.program_id(1) == 0)
    def _():
        _rmsnorm_to(h_ref, x_ref, g_ref, IN_TM)

    for k in range(IN_TN // SLAB):
        r = jnp.dot(h_ref[...], w_ref[:, k * SLAB:(k + 1) * SLAB], preferred_element_type=F32)
        o_ref[k] = r.astype(o_ref.dtype)


def _in_projection(x2, g, w_perm_all, layer):
    rows, d = x2.shape
    return pl.pallas_call(
        _inproj_kernel,
        grid=(rows // IN_TM, N_IN_PAD // IN_TN),
        in_specs=[pl.BlockSpec((IN_TM, d), lambda i, j: (i, 0)),
                  pl.BlockSpec((1, d), lambda i, j: (0, 0)),
                  pl.BlockSpec((None, d, IN_TN), lambda i, j: (layer, 0, j))],
        out_specs=pl.BlockSpec((IN_TN // SLAB, IN_TM, SLAB), lambda i, j: (j, i, 0)),
        out_shape=jax.ShapeDtypeStruct((N_SLABS, rows, SLAB), BF16),
        scratch_shapes=[pltpu.VMEM((IN_TM, d), BF16)],
        compiler_params=_cparams(("parallel", "arbitrary")),
        name="in_projection",
    )(x2, g.reshape(1, d), w_perm_all)


MIX_TS = 256
MIX_TR = MIX_TS * BATCH
CONV_HALO = 256
CONV_RC = 64
CONV_PARTS = 4


def _conv_kernel(v_ref, g_ref, w_ref, b_ref, lg_ref, lb_ref, o_ref, ubuf, ybuf):
    nc = D_CONV // LANES
    per_slab = SLAB // LANES

    @pl.when(pl.program_id(0) == 0)
    def _():
        ubuf[:, 0:CONV_HALO, :] = jnp.zeros((nc, CONV_HALO, LANES), F32)

    def glu(b, carry):
        for c in range(nc):
            k, ls = c // per_slab, slice((c % per_slab) * LANES, (c % per_slab + 1) * LANES)
            v = v_ref[k, b, :, ls].astype(F32)
            g = g_ref[k, b, :, ls].astype(F32)
            ubuf[c, pl.ds(CONV_HALO + b, MIX_TS, stride=BATCH), :] = v * _sigmoid(g)
        return carry

    lax.fori_loop(0, BATCH, glu, 0)

    base = CONV_HALO - (CONV_WIDTH - 1) * BATCH

    def conv(i, carry):
        r0 = pl.multiple_of(i * CONV_RC, CONV_RC)
        rows = pl.ds(r0, CONV_RC)
        total = jnp.zeros((CONV_RC, 1), F32)
        for c in range(nc):
            ls = slice(c * LANES, (c + 1) * LANES)
            parts = [None] * CONV_PARTS
            for j in range(CONV_WIDTH):
                term = w_ref[j:j + 1, ls] * ubuf[c, pl.ds(r0 + base + BATCH * j, CONV_RC), :]
                k = j % CONV_PARTS
                parts[k] = term if parts[k] is None else parts[k] + term
            a = (parts[0] + parts[1]) + (parts[2] + parts[3]) + b_ref[:, ls]
            ybuf[c, rows, :] = a
            total = total + jnp.sum(a, axis=-1, keepdims=True)
        mu = total * (1.0 / D_CONV)
        sq = jnp.zeros((CONV_RC, 1), F32)
        for c in range(nc):
            cen = ybuf[c, rows, :] - mu
            sq = sq + jnp.sum(cen * cen, axis=-1, keepdims=True)
        inv = lax.rsqrt(sq * (1.0 / D_CONV) + NORM_EPS)
        for c in range(nc):
            ls = slice(c * LANES, (c + 1) * LANES)
            y = (ybuf[c, rows, :] - mu) * inv * lg_ref[:, ls] + lb_ref[:, ls]
            ybuf[c, rows, :] = y * _sigmoid(y)
        return carry

    lax.fori_loop(0, MIX_TR // CONV_RC, conv, 0)

    def put(b, carry):
        for c in range(nc):
            k, ls = c // per_slab, slice((c % per_slab) * LANES, (c % per_slab + 1) * LANES)
            o_ref[k, b, :, ls] = ybuf[c, pl.ds(b, MIX_TS, stride=BATCH), :].astype(o_ref.dtype)
        return carry

    lax.fori_loop(0, BATCH, put, 0)
    ubuf[:, 0:CONV_HALO, :] = ubuf[:, MIX_TR:MIX_TR + CONV_HALO, :]


def _conv_branch(proj, w, b, lg, lb):
    seq = proj.shape[1] // BATCH
    p4 = proj.reshape(N_SLABS, BATCH, seq, SLAB)
    nk = D_CONV // SLAB
    vec = lambda: pl.BlockSpec((1, D_CONV), lambda i: (0, 0))
    out = pl.pallas_call(
        _conv_kernel,
        grid=(seq // MIX_TS,),
        in_specs=[pl.BlockSpec((nk, BATCH, MIX_TS, SLAB), lambda i: (SL_AV // nk, 0, i, 0)),
                  pl.BlockSpec((nk, BATCH, MIX_TS, SLAB), lambda i: (SL_AG // nk, 0, i, 0)),
                  pl.BlockSpec((CONV_WIDTH, D_CONV), lambda i: (0, 0)),
                  vec(), vec(), vec()],
        out_specs=pl.BlockSpec((nk, BATCH, MIX_TS, SLAB), lambda i: (0, 0, i, 0)),
        out_shape=jax.ShapeDtypeStruct((nk, BATCH, seq, SLAB), BF16),
        scratch_shapes=[pltpu.VMEM((D_CONV // LANES, CONV_HALO + MIX_TR, LANES), F32),
                        pltpu.VMEM((D_CONV // LANES, MIX_TR, LANES), F32)],
        compiler_params=_cparams(("arbitrary",)),
        name="conv_branch",
    )(p4, p4, w, b.reshape(1, -1), lg.reshape(1, -1), lb.reshape(1, -1))
    return out.reshape(nk, BATCH * seq, SLAB)


RNN_HALO = 32
RNN_RC = 256


def _rglru_kernel(x_ref, gate_ref, cw_ref, cb_ref, wa_ref, ba_ref, wx_ref, bx_ref, lam_ref,
                  o_ref, xbuf, abuf, gbuf, hstate):
    per_slab = SLAB // RNN_BLOCK_W

    @pl.when(pl.program_id(0) == 0)
    def _():
        xbuf[:, 0:RNN_HALO, :] = jnp.zeros((RNN_BLOCKS, RNN_HALO, RNN_BLOCK_W), F32)
        hstate[...] = jnp.zeros_like(hstate)

    def slab_cols(n):
        return n // per_slab, slice((n % per_slab) * RNN_BLOCK_W, (n % per_slab + 1) * RNN_BLOCK_W)

    def load(b, carry):
        for n in range(RNN_BLOCKS):
            k, ls = slab_cols(n)
            xbuf[n, pl.ds(RNN_HALO + b, MIX_TS, stride=BATCH), :] = x_ref[k, b, :, ls].astype(F32)
        return carry

    lax.fori_loop(0, BATCH, load, 0)

    z = -lam_ref[...]
    softplus = jnp.maximum(z, 0.0) + jnp.log(1.0 + jnp.exp(-jnp.abs(z)))
    coef = -RG_C * softplus
    base = RNN_HALO - (RNN_CONV_WIDTH - 1) * BATCH

    def gates(i, carry):
        r0 = pl.multiple_of(i * RNN_RC, RNN_RC)
        for n in range(RNN_BLOCKS):
            cs = slice(n * RNN_BLOCK_W, (n + 1) * RNN_BLOCK_W)
            y = jnp.zeros((RNN_RC, RNN_BLOCK_W), F32)
            for j in range(RNN_CONV_WIDTH):
                y = y + cw_ref[j:j + 1, cs] * xbuf[n, pl.ds(r0 + base + BATCH * j, RNN_RC), :]
            y = y + cb_ref[:, cs]
            yb = y.astype(BF16)
            ra = _sigmoid(jnp.dot(yb, wa_ref[n], preferred_element_type=F32) + ba_ref[:, cs])
            ri = _sigmoid(jnp.dot(yb, wx_ref[n], preferred_element_type=F32) + bx_ref[:, cs])
            a = jnp.exp(coef[:, cs] * ra)
            abuf[n, pl.ds(r0, RNN_RC), :] = a
            gbuf[n, pl.ds(r0, RNN_RC), :] = jnp.sqrt(1.0 - a * a) * (ri * y)
        return carry

    lax.fori_loop(0, MIX_TR // RNN_RC, gates, 0)

    def step(t, h):
        r0 = pl.multiple_of(t * BATCH, BATCH)
        h = abuf[:, pl.ds(r0, BATCH), :] * h + gbuf[:, pl.ds(r0, BATCH), :]
        gbuf[:, pl.ds(r0, BATCH), :] = h
        return h

    hstate[...] = lax.fori_loop(0, MIX_TS, step, hstate[...], unroll=8)

    def put(b, carry):
        for n in range(RNN_BLOCKS):
            k, ls = slab_cols(n)
            h = gbuf[n, pl.ds(b, MIX_TS, stride=BATCH), :]
            o_ref[k, b, :, ls] = (h * _gelu_tanh(gate_ref[k, b, :, ls].astype(F32))).astype(o_ref.dtype)
        return carry

    lax.fori_loop(0, BATCH, put, 0)
    xbuf[:, 0:RNN_HALO, :] = xbuf[:, MIX_TR:MIX_TR + RNN_HALO, :]


def _rglru_branch(proj, cw, cb, wa, ba, wx, bx, lam):
    seq = proj.shape[1] // BATCH
    p4 = proj.reshape(N_SLABS, BATCH, seq, SLAB)
    nk = D_RNN // SLAB
    vec = lambda: pl.BlockSpec((1, D_RNN), lambda i: (0, 0))
    blk = lambda: pl.BlockSpec((RNN_BLOCKS, RNN_BLOCK_W, RNN_BLOCK_W), lambda i: (0, 0, 0))
    out = pl.pallas_call(
        _rglru_kernel,
        grid=(seq // MIX_TS,),
        in_specs=[pl.BlockSpec((nk, BATCH, MIX_TS, SLAB), lambda i: (SL_RX // nk, 0, i, 0)),
                  pl.BlockSpec((nk, BATCH, MIX_TS, SLAB), lambda i: (SL_RG // nk, 0, i, 0)),
                  pl.BlockSpec((RNN_CONV_WIDTH, D_RNN), lambda i: (0, 0)),
                  vec(), blk(), vec(), blk(), vec(), vec()],
        out_specs=pl.BlockSpec((nk, BATCH, MIX_TS, SLAB), lambda i: (0, 0, i, 0)),
        out_shape=jax.ShapeDtypeStruct((nk, BATCH, seq, SLAB), BF16),
        scratch_shapes=[pltpu.VMEM((RNN_BLOCKS, RNN_HALO + MIX_TR, RNN_BLOCK_W), F32),
                        pltpu.VMEM((RNN_BLOCKS, MIX_TR, RNN_BLOCK_W), F32),
                        pltpu.VMEM((RNN_BLOCKS, MIX_TR, RNN_BLOCK_W), F32),
                        pltpu.VMEM((RNN_BLOCKS, BATCH, RNN_BLOCK_W), F32)],
        compiler_params=_cparams(("arbitrary",)),
        name="rglru_branch",
    )(p4, p4, cw, cb.reshape(1, -1), wa.astype(BF16), ba.reshape(1, -1),
      wx.astype(BF16), bx.reshape(1, -1), lam.reshape(1, -1))
    return out.reshape(nk, BATCH * seq, SLAB)


def _compress_kernel(k_ref, v_ref, pe_ref, kw1_ref, vw1_ref, kw2_ref, vw2_ref, kc_ref, vc_ref,
                     stage, acc, pacc):
    for src, w1_ref, w2_ref, dst in ((k_ref, kw1_ref, kw2_ref, kc_ref),
                                     (v_ref, vw1_ref, vw2_ref, vc_ref)):
        for h in range(N_KV_HEADS):
            stage[h] = src[:, h * HEAD_DIM:(h + 1) * HEAD_DIM].astype(F32)
        acc[...] = jnp.zeros_like(acc)
        pacc[...] = jnp.zeros_like(pacc)

        def body(l, carry):
            w1 = w1_ref[l]
            for h in range(N_KV_HEADS):
                x = stage[h, pl.ds(l, N_CMP_PAD, stride=CMP_STRIDE), :]
                acc[h] += jnp.dot(x.astype(BF16), w1, preferred_element_type=F32)
            pe = pe_ref[l].astype(BF16)
            pacc[...] += (jnp.dot(pe[:, :HEAD_DIM], w1[:, :HEAD_DIM], preferred_element_type=F32)
                          + jnp.dot(pe[:, HEAD_DIM:], w1[:, HEAD_DIM:], preferred_element_type=F32))
            return carry

        lax.fori_loop(0, CMP_STRIDE, body, 0)

        for h in range(N_KV_HEADS):
            p = acc[h]
            hi_next = pltpu.roll(p[:, HEAD_DIM:], N_CMP_PAD - 1, 0)
            pre = p[:, :HEAD_DIM] + hi_next + pacc[0:1, :]
            y = jnp.dot(_gelu_tanh(pre).astype(BF16), w2_ref[...], preferred_element_type=F32)
            dst[:, h * HEAD_DIM:(h + 1) * HEAD_DIM] = y.astype(dst.dtype)


def _compress(proj, pe, kw1, kw2, vw1, vw2):
    seq = proj.shape[1] // BATCH
    pe2 = jnp.zeros((CMP_STRIDE, 8, 2 * HEAD_DIM), F32)
    pe2 = pe2.at[:, 0, :HEAD_DIM].set(pe[:CMP_STRIDE]).at[:, 0, HEAD_DIM:].set(pe[CMP_STRIDE:])

    def w1cat(w1):
        w = w1.reshape(2, CMP_STRIDE, HEAD_DIM, HEAD_DIM)
        return jnp.concatenate([w[0], w[1]], axis=-1).astype(BF16)

    kv_spec = lambda sl: pl.BlockSpec((None, seq, SLAB), lambda b: (sl, b, 0))
    w1_spec = lambda: pl.BlockSpec((CMP_STRIDE, HEAD_DIM, 2 * HEAD_DIM), lambda b: (0, 0, 0))
    w2_spec = lambda: pl.BlockSpec((HEAD_DIM, HEAD_DIM), lambda b: (0, 0))
    out_spec = lambda: pl.BlockSpec((None, N_CMP_PAD, KV_W), lambda b: (b, 0, 0))
    return pl.pallas_call(
        _compress_kernel,
        grid=(BATCH,),
        in_specs=[kv_spec(SL_KC), kv_spec(SL_VC),
                  pl.BlockSpec((CMP_STRIDE, 8, 2 * HEAD_DIM), lambda b: (0, 0, 0)),
                  w1_spec(), w1_spec(), w2_spec(), w2_spec()],
        out_specs=[out_spec(), out_spec()],
        out_shape=[jax.ShapeDtypeStruct((BATCH, N_CMP_PAD, KV_W), BF16)] * 2,
        scratch_shapes=[pltpu.VMEM((N_KV_HEADS, seq, HEAD_DIM), F32),
                        pltpu.VMEM((N_KV_HEADS, N_CMP_PAD, 2 * HEAD_DIM), F32),
                        pltpu.VMEM((8, HEAD_DIM), F32)],
        compiler_params=_cparams(("parallel",)),
        name="compress_kv",
    )(proj, proj, pe2, w1cat(kw1), w1cat(vw1), kw2.astype(BF16), vw2.astype(BF16))


ATT_TQ = 256
ATT_TK = 256
ATT_NKB = SEQ // ATT_TK
ATT_WB = WINDOW // ATT_TK + 1
ATT_TS = 2 * ATT_TK
ATT_LANES = GROUP * ATT_TQ


def _tile_heads(x):
    return jnp.concatenate([x] * GROUP, axis=1)


def _attn_kernel(q_ref, ks_ref, vs_ref, kw_ref, vw_ref, kc_ref, vc_ref, cg_ref, cos_ref, sin_ref,
                 cost_ref, sint_ref, ovt_ref, o_ref, ksr, kwr, vst, vwt, vct, qaug, score_scr):
    qi = pl.program_id(1)
    scale = np.float32(HEAD_DIM ** -0.5 * np.log2(np.e))
    half = HEAD_DIM // 2

    @pl.when(qi == 0)
    def _():
        def prep(c, carry):
            r0 = pl.multiple_of(c * ATT_TK, ATT_TK)
            cos = cos_ref[pl.ds(r0, ATT_TK), :]
            sin = sin_ref[pl.ds(r0, ATT_TK), :]
            blk = (r0 + lax.broadcasted_iota(jnp.int32, (ATT_TK, HEAD_DIM), 0)) >> 6
            onehot = jnp.where(lax.broadcasted_iota(jnp.int32, (ATT_TK, HEAD_DIM), 1) == blk, 1.0, 0.0)
            for h in range(N_KV_HEADS):
                cs = slice(h * HEAD_DIM, (h + 1) * HEAD_DIM)
                x = ks_ref[pl.ds(r0, ATT_TK), cs].astype(F32)
                ksr[h, pl.ds(r0, ATT_TK), 0:HEAD_DIM] = (
                    x * cos + pltpu.roll(x, half, 1) * sin).astype(ksr.dtype)
                ksr[h, pl.ds(r0, ATT_TK), HEAD_DIM:2 * HEAD_DIM] = onehot.astype(ksr.dtype)
                x = kw_ref[pl.ds(r0, ATT_TK), cs].astype(F32)
                kwr[pl.ds(r0, ATT_TK), cs] = (x * cos + pltpu.roll(x, half, 1) * sin).astype(kwr.dtype)
                vwt[h, c] = vw_ref[pl.ds(r0, ATT_TK), cs].astype(F32).T.astype(vwt.dtype)
            return carry

        lax.fori_loop(0, ATT_NKB, prep, 0)

        def prep_vs(c, carry):
            for h in range(N_KV_HEADS):
                cs = slice(h * HEAD_DIM, (h + 1) * HEAD_DIM)
                parts = []
                for i in range(ATT_TS // ATT_TK):
                    r0 = pl.multiple_of(c * ATT_TS + i * ATT_TK, ATT_TK)
                    parts.append(vs_ref[pl.ds(r0, ATT_TK), cs].astype(F32).T.astype(vst.dtype))
                vst[h, c] = jnp.concatenate(parts, axis=1)
            return carry

        lax.fori_loop(0, SEQ // ATT_TS, prep_vs, 0)
        for h in range(N_KV_HEADS):
            vct[h] = vc_ref[:, h * HEAD_DIM:(h + 1) * HEAD_DIM].astype(F32).T.astype(vct.dtype)
        qaug[:, HEAD_DIM + N_SEL:, :] = jnp.zeros(
            (N_KV_HEADS, HEAD_DIM - N_SEL, ATT_LANES), qaug.dtype)

    t0 = qi * ATT_TQ
    cos_t = cost_ref[...]
    sin_t = sint_ref[...]
    gate_t = _sigmoid(cg_ref[:, 0:HEAD_DIM].astype(F32).T)

    t_c = t0 + (lax.broadcasted_iota(jnp.int32, (N_CMP_PAD, ATT_LANES), 1) & (ATT_TQ - 1))
    n_sub = lax.broadcasted_iota(jnp.int32, (N_CMP_PAD, ATT_LANES), 0)
    valid_c = (n_sub * CMP_STRIDE + (CMP_BLOCK - 1)) <= t_c
    t_s = t0 + lax.broadcasted_iota(jnp.int32, (N_SEL, ATT_TQ), 1)
    m_sub = lax.broadcasted_iota(jnp.int32, (N_SEL, ATT_TQ), 0)
    cur = t_s >> 6
    valid_s = m_sub <= cur
    forced = (m_sub == 0) | (m_sub == cur) | (m_sub == cur - 1)
    w_blk = jnp.maximum(qi - WINDOW // ATT_TK, 0)
    w_start = pl.multiple_of(w_blk * ATT_TK, ATT_TK)
    dist = ((t0 + lax.broadcasted_iota(jnp.int32, (ATT_WB * ATT_TK, ATT_TQ), 1))
            - (w_start + lax.broadcasted_iota(jnp.int32, (ATT_WB * ATT_TK, ATT_TQ), 0)))
    win_bias = jnp.where((dist >= 0) & (dist < WINDOW), 0.0, NEG_INF)

    o_cmp = []
    for h in range(N_KV_HEADS):
        hs = slice(h * HEAD_DIM, (h + 1) * HEAD_DIM)
        q_plain, q_rot = [], []
        for g in range(GROUP):
            hq = h * GROUP + g
            xt = q_ref[hq // 2, :, (hq % 2) * HEAD_DIM:(hq % 2 + 1) * HEAD_DIM].astype(F32).T
            swapped = jnp.concatenate([xt[half:], xt[:half]], axis=0)
            q_plain.append((xt * scale).astype(BF16))
            q_rot.append(((xt * cos_t + swapped * sin_t) * scale).astype(BF16))
        qn3 = jnp.concatenate(q_plain, axis=1)
        qaug[h, 0:HEAD_DIM, :] = jnp.concatenate(q_rot, axis=1)

        s = jnp.where(valid_c, jnp.dot(kc_ref[:, hs], qn3, preferred_element_type=F32), NEG_INF)
        mx = jnp.max(s, axis=0, keepdims=True)
        e = jnp.where(valid_c, jnp.exp2(s - mx), 0.0)
        den = jnp.sum(e, axis=0, keepdims=True)
        p = e / jnp.where(den > 0.0, den, 1.0)
        o_cmp.append(jnp.dot(vct[h], p.astype(BF16), preferred_element_type=F32))
        p_sum = p[:, 0:ATT_TQ]
        for g in range(1, GROUP):
            p_sum = p_sum + p[:, g * ATT_TQ:(g + 1) * ATT_TQ]

        imp = jnp.dot(ovt_ref[...], p_sum, preferred_element_type=F32,
                      precision=lax.Precision.HIGHEST)
        score = jnp.where(valid_s, jnp.where(forced, POS_INF, imp), NEG_INF)
        score_scr[...] = score
        m_v = lax.broadcasted_iota(jnp.int32, (8, ATT_TQ), 0)
        ranks = []
        for v in range(N_SEL // 8):
            sc_v = score_scr[8 * v:8 * v + 8, :]
            cnt = jnp.zeros((8, ATT_TQ), F32)
            for mp in range(N_SEL):
                row = score_scr[mp:mp + 1, :]
                if mp < 8 * v:
                    beats = row >= sc_v
                elif mp >= 8 * v + 8:
                    beats = row > sc_v
                else:
                    beats = (row > sc_v) | ((row == sc_v) & (m_v > mp - 8 * v))
                cnt = cnt + jnp.where(beats, 1.0, 0.0)
            keep = (cnt < SEL_TOP_N) & (sc_v > NEG_INF)
            ranks.append(jnp.where(keep, 0.0, NEG_INF))
        sel_bias = jnp.concatenate(ranks, axis=0)
        qaug[h, HEAD_DIM:HEAD_DIM + N_SEL, :] = _tile_heads(sel_bias).astype(qaug.dtype)

    def sel_step(j, carry, causal):
        k0 = pl.multiple_of(j * ATT_TS, ATT_TS)
        if causal:
            kpos = k0 + lax.broadcasted_iota(jnp.int32, (ATT_TS, ATT_TQ), 0)
            tpos = t0 + lax.broadcasted_iota(jnp.int32, (ATT_TS, ATT_TQ), 1)
            causal_bias = _tile_heads(jnp.where(kpos <= tpos, 0.0, NEG_INF))
        out = []
        for h in range(N_KV_HEADS):
            m_run, l_run, acc = carry[h]
            sc = jnp.dot(ksr[h, pl.ds(k0, ATT_TS), :], qaug[h], preferred_element_type=F32)
            if causal:
                sc = sc + causal_bias
            m_new = jnp.maximum(m_run, jnp.max(sc, axis=0, keepdims=True))
            alpha = jnp.exp2(m_run - m_new)
            pr = jnp.exp2(sc - m_new)
            l_new = alpha * l_run + jnp.sum(pr, axis=0, keepdims=True)
            acc = alpha * acc + jnp.dot(vst[h, j], pr.astype(BF16), preferred_element_type=F32)
            out.append((m_new, l_new, acc))
        return tuple(out)

    init = (jnp.full((1, ATT_LANES), NEG_INF, F32), jnp.zeros((1, ATT_LANES), F32),
            jnp.zeros((HEAD_DIM, ATT_LANES), F32))
    last = (t0 + ATT_TQ - 1) // ATT_TS
    carry = lax.fori_loop(0, last, lambda j, c: sel_step(j, c, False), (init,) * N_KV_HEADS)
    sel_out = sel_step(last, carry, True)

    for h in range(N_KV_HEADS):
        hs = slice(h * HEAD_DIM, (h + 1) * HEAD_DIM)
        _, l_s, acc_s = sel_out[h]
        o_sel = acc_s / l_s

        sw = (jnp.dot(kwr[pl.ds(w_start, ATT_WB * ATT_TK), hs], qaug[h, 0:HEAD_DIM, :],
                      preferred_element_type=F32) + _tile_heads(win_bias))
        mw = jnp.max(sw, axis=0, keepdims=True)
        pw = jnp.exp2(sw - mw)
        den_w = jnp.sum(pw, axis=0, keepdims=True)
        pw = pw.astype(BF16)
        o_win = None
        for i in range(ATT_WB):
            part = jnp.dot(vwt[h, w_blk + i], pw[i * ATT_TK:(i + 1) * ATT_TK],
                           preferred_element_type=F32)
            o_win = part if o_win is None else o_win + part
        o_win = o_win / den_w

        for g in range(GROUP):
            hq = h * GROUP + g
            ls = slice(g * ATT_TQ, (g + 1) * ATT_TQ)
            out_t = (gate_t[3 * hq:3 * hq + 1, :] * o_cmp[h][:, ls]
                     + gate_t[3 * hq + 1:3 * hq + 2, :] * o_sel[:, ls]
                     + gate_t[3 * hq + 2:3 * hq + 3, :] * o_win[:, ls])
            o_ref[hq // 2, :, (hq % 2) * HEAD_DIM:(hq % 2 + 1) * HEAD_DIM] = out_t.T.astype(o_ref.dtype)


def _attention(proj, kc, vc, tables):
    cos, sin, cos_t, sin_t = tables
    rows = proj.shape[1]
    seq = rows // BATCH
    nq = seq // ATT_TQ
    c_start = np.arange(N_CMP_PAD) * CMP_STRIDE
    s_start = np.arange(N_SEL) * SEL_BLOCK
    ovt = ((c_start[None, :] < s_start[:, None] + SEL_BLOCK)
           & (c_start[None, :] + CMP_BLOCK > s_start[:, None])
           & (np.arange(N_CMP_PAD)[None, :] < N_CMP_PAD - 1)).astype(np.float32)

    full = lambda sl: pl.BlockSpec((None, seq, SLAB), lambda b, qi: (sl, b, 0))
    cmp_spec = lambda: pl.BlockSpec((None, N_CMP_PAD, KV_W), lambda b, qi: (b, 0, 0))
    tab = lambda: pl.BlockSpec((seq, HEAD_DIM), lambda b, qi: (0, 0))
    tab_t = lambda: pl.BlockSpec((None, HEAD_DIM, ATT_TQ), lambda b, qi: (qi, 0, 0))
    nq3 = D_ATTN // SLAB
    kv_t = lambda tk: pltpu.VMEM((N_KV_HEADS, seq // tk, HEAD_DIM, tk), BF16)
    return pl.pallas_call(
        _attn_kernel,
        grid=(BATCH, nq),
        in_specs=[pl.BlockSpec((nq3, ATT_TQ, SLAB), lambda b, qi: (SL_Q // nq3, b * nq + qi, 0)),
                  full(SL_KS), full(SL_VS), full(SL_KW), full(SL_VW),
                  cmp_spec(), cmp_spec(),
                  pl.BlockSpec((None, ATT_TQ, SLAB), lambda b, qi: (SL_CG, b * nq + qi, 0)),
                  tab(), tab(), tab_t(), tab_t(),
                  pl.BlockSpec((N_SEL, N_CMP_PAD), lambda b, qi: (0, 0))],
        out_specs=pl.BlockSpec((nq3, ATT_TQ, SLAB), lambda b, qi: (0, b * nq + qi, 0)),
        out_shape=jax.ShapeDtypeStruct((nq3, rows, SLAB), BF16),
        scratch_shapes=[pltpu.VMEM((N_KV_HEADS, seq, 2 * HEAD_DIM), BF16),
                        pltpu.VMEM((seq, KV_W), BF16),
                        kv_t(ATT_TS), kv_t(ATT_TK),
                        pltpu.VMEM((N_KV_HEADS, HEAD_DIM, N_CMP_PAD), BF16),
                        pltpu.VMEM((N_KV_HEADS, 2 * HEAD_DIM, ATT_LANES), BF16),
                        pltpu.VMEM((N_SEL, ATT_TQ), F32)],
        compiler_params=_cparams(("parallel", "arbitrary")),
        name="sparse_attention",
    )(proj, proj, proj, proj, proj, kc, vc, proj, cos, sin, cos_t, sin_t, jnp.asarray(ovt))


MERGE_TM = 512
MERGE_TN = 512


def _cat_slabs(ref, first, n):
    return jnp.concatenate([ref[first + k] for k in range(n)], axis=-1)


def _merge_kernel(ua_ref, ub_ref, uc_ref, ga_ref, gb_ref, gc_ref, x_ref,
                  wa_ref, wb_ref, wc_ref, wo_ref, o_ref, y_ref):
    per = MERGE_TN // SLAB
    branches = ((ua_ref, ga_ref, wa_ref), (ub_ref, gb_ref, wb_ref), (uc_ref, gc_ref, wc_ref))
    acts = [_cat_slabs(u_ref, 0, u_ref.shape[0]) for u_ref, _, _ in branches]
    for c in range(D_MODEL // MERGE_TN):
        cs = slice(c * MERGE_TN, (c + 1) * MERGE_TN)
        y = None
        for u, (_, g_ref, w_ref) in zip(acts, branches):
            p = jnp.dot(u, w_ref[:, cs], preferred_element_type=F32)
            term = _sigmoid(_cat_slabs(g_ref, c * per, per).astype(F32)) * p
            y = term if y is None else y + term
        y_ref[:, cs] = y.astype(y_ref.dtype)
    o_ref[...] = x_ref[...] + jnp.dot(y_ref[...], wo_ref[...], preferred_element_type=F32)


def _merge(ua, ub, uc, proj, x2, wa, wb, wc, wo):
    rows, d = x2.shape
    ng = d // SLAB
    act = lambda n: pl.BlockSpec((n, MERGE_TM, SLAB), lambda i: (0, i, 0))
    gate = lambda sl: pl.BlockSpec((ng, MERGE_TM, SLAB), lambda i: (sl // ng, i, 0))
    res = lambda k: pl.BlockSpec((k, d), lambda i: (0, 0), pipeline_mode=pl.Buffered(1))
    return pl.pallas_call(
        _merge_kernel,
        grid=(rows // MERGE_TM,),
        in_specs=[act(ua.shape[0]), act(ub.shape[0]), act(uc.shape[0]),
                  gate(SL_GA), gate(SL_GB), gate(SL_GC),
                  pl.BlockSpec((MERGE_TM, d), lambda i: (i, 0)),
                  res(wa.shape[0]), res(wb.shape[0]), res(wc.shape[0]), res(wo.shape[0])],
        out_specs=pl.BlockSpec((MERGE_TM, d), lambda i: (i, 0)),
        out_shape=jax.ShapeDtypeStruct((rows, d), F32),
        scratch_shapes=[pltpu.VMEM((MERGE_TM, d), BF16)],
        compiler_params=_cparams(("parallel",)),
        name="merge_out_proj",
    )(ua, ub, uc, proj, proj, proj, x2, wa, wb, wc, wo)


MLP_TM = 1024
MLP_TF = 512


def _mlp_kernel(x_ref, g_ref, wu_ref, wd_ref, og_ref, o_ref, h_ref, *, norm_output):
    @pl.when(pl.program_id(1) == 0)
    def _():
        _rmsnorm_to(h_ref, x_ref, g_ref, MLP_TM)
        o_ref[...] = x_ref[...]

    a = jnp.dot(h_ref[...], wu_ref[...], preferred_element_type=F32)
    a = jnp.maximum(a, 0.0)
    o_ref[...] += jnp.dot((a * a).astype(BF16), wd_ref[...], preferred_element_type=F32)

    if norm_output:
        @pl.when(pl.program_id(1) == pl.num_programs(1) - 1)
        def _():
            _rmsnorm_to(o_ref, o_ref, og_ref, MLP_TM)


def _mlp(x2, g, wu, wd, out_g, norm_output):
    rows, d = x2.shape
    f = wu.shape[1]
    return pl.pallas_call(
        functools.partial(_mlp_kernel, norm_output=norm_output),
        grid=(rows // MLP_TM, f // MLP_TF),
        in_specs=[pl.BlockSpec((MLP_TM, d), lambda i, j: (i, 0)),
                  pl.BlockSpec((1, d), lambda i, j: (0, 0)),
                  pl.BlockSpec((d, MLP_TF), lambda i, j: (0, j)),
                  pl.BlockSpec((MLP_TF, d), lambda i, j: (j, 0)),
                  pl.BlockSpec((1, d), lambda i, j: (0, 0))],
        out_specs=pl.BlockSpec((MLP_TM, d), lambda i, j: (i, 0)),
        out_shape=jax.ShapeDtypeStruct((rows, d), F32),
        scratch_shapes=[pltpu.VMEM((MLP_TM, d), BF16)],
        compiler_params=_cparams(("parallel", "arbitrary")),
        name="mlp",
    )(x2, g.reshape(1, d), wu, wd, out_g.reshape(1, d))


def _prep_w_in_kernel(a_ref, b_ref, o_ref):
    j = pl.program_id(0)
    shift = _O_GA % SLAB
    n_gate = 3 * N_Q_HEADS
    n_layers = a_ref.shape[1]

    @pl.when(j < SL_RX)
    def _():
        for l in range(n_layers):
            at = a_ref[:, l, :].T
            bt = b_ref[:, l, :].T
            o_ref[l] = jnp.concatenate([at[:, shift:], bt[:, :shift]], axis=1).astype(o_ref.dtype)

    @pl.when((j >= SL_RX) & (j < SL_CG))
    def _():
        for l in range(n_layers):
            o_ref[l] = a_ref[:, l, :].T.astype(o_ref.dtype)

    @pl.when(j == SL_CG)
    def _():
        lane = lax.broadcasted_iota(jnp.int32, (D_MODEL, SLAB), 1)
        for l in range(n_layers):
            o_ref[l] = jnp.where(lane < n_gate, a_ref[:, l, :].T, 0.0).astype(o_ref.dtype)


def _permute_w_in_all(w_all):
    n_layers, d, n_in = w_all.shape
    w_t = jnp.transpose(w_all, (2, 0, 1))

    def src_block(j):
        blk = (_O_GA // SLAB) + j
        blk = jnp.where(j >= SL_RX, _O_RX // SLAB + (j - SL_RX), blk)
        blk = jnp.where(j >= SL_Q, _O_Q // SLAB + (j - SL_Q), blk)
        blk = jnp.where(j >= SL_AV, _O_AV // SLAB + (j - SL_AV), blk)
        blk = jnp.where(j >= SL_VS, _O_VS // SLAB + (j - SL_VS), blk)
        return jnp.where(j >= SL_CG, _O_CG // SLAB, blk)

    last_blk = (n_in - 1) // SLAB
    return pl.pallas_call(
        _prep_w_in_kernel,
        grid=(N_SLABS,),
        in_specs=[pl.BlockSpec((SLAB, n_layers, d), lambda j: (src_block(j), 0, 0)),
                  pl.BlockSpec((SLAB, n_layers, d),
                               lambda j: (jnp.where(j < SL_RX, src_block(j) + 1, last_blk), 0, 0))],
        out_specs=pl.BlockSpec((n_layers, d, SLAB), lambda j: (0, 0, j)),
        out_shape=jax.ShapeDtypeStruct((n_layers, d, N_IN_PAD), BF16),
        compiler_params=_cparams(("parallel",)),
        name="prep_w_in",
    )(w_t, w_t)


CAST_TR = 256


def _cast_kernel(w_ref, o_ref):
    o_ref[...] = w_ref[...].astype(o_ref.dtype)


def _cast_bf16(w_all, layer):
    _, r, c = w_all.shape
    tr = min(CAST_TR, r)
    return pl.pallas_call(
        _cast_kernel,
        grid=(r // tr,),
        in_specs=[pl.BlockSpec((None, tr, c), lambda i: (layer, i, 0))],
        out_specs=pl.BlockSpec((tr, c), lambda i: (i, 0)),
        out_shape=jax.ShapeDtypeStruct((r, c), BF16),
        compiler_params=_cparams(("parallel",)),
        name="cast_bf16",
    )(w_all)


def _rope_tables(s):
    inv = 1.0 / (ROPE_THETA ** (jnp.arange(0, HEAD_DIM, 2, dtype=F32) / HEAD_DIM))
    ang = jnp.arange(s, dtype=F32)[:, None] * inv[None, :]
    cos, sin = jnp.cos(ang), jnp.sin(ang)
    cos_f = jnp.concatenate([cos, cos], axis=-1)
    sin_f = jnp.concatenate([-sin, sin], axis=-1)
    tiles = lambda a: a.reshape(s // ATT_TQ, ATT_TQ, HEAD_DIM).transpose(0, 2, 1)
    return cos_f, sin_f, tiles(cos_f), tiles(sin_f)


def _layer(x2, tables, w_in_bf_all, layer, w_mats_bf, final_norm_g, attn_norm_g, conv_dw_w, conv_dw_b,
           conv_ln_g, conv_ln_b, rnn_conv_w, rnn_conv_b, rglru_wa, rglru_ba, rglru_wx, rglru_bx,
           rglru_lambda, cmp_pe, cmp_k_w1, cmp_k_w2, cmp_v_w1, cmp_v_w2, mlp_norm_g):
    w_conv_out, w_rnn_out, w_attn_out, w_o, w_mlp_up, w_mlp_down = w_mats_bf
    is_last = layer == DEPTH - 1
    proj = _in_projection(x2, attn_norm_g, w_in_bf_all, layer)
    ua = _conv_branch(proj, conv_dw_w, conv_dw_b, conv_ln_g, conv_ln_b)
    ub = _rglru_branch(proj, rnn_conv_w, rnn_conv_b, rglru_wa, rglru_ba, rglru_wx, rglru_bx,
                       rglru_lambda)
    kc, vc = _compress(proj, cmp_pe, cmp_k_w1, cmp_k_w2, cmp_v_w1, cmp_v_w2)
    uc = _attention(proj, kc, vc, tables)
    x2 = _merge(ua, ub, uc, proj, x2, w_conv_out, w_rnn_out, w_attn_out, w_o)
    return _mlp(x2, mlp_norm_g, w_mlp_up, w_mlp_down, final_norm_g, is_last)


def kernel(x, attn_norm_g, w_in, conv_dw_w, conv_dw_b, conv_ln_g, conv_ln_b, w_conv_out, rnn_conv_w, rnn_conv_b, rglru_wa, rglru_ba, rglru_wx, rglru_bx, rglru_lambda, w_rnn_out, cmp_pe, cmp_k_w1, cmp_k_w2, cmp_v_w1, cmp_v_w2, w_attn_out, w_o, mlp_norm_g, w_mlp_up, w_mlp_down, final_norm_g):
    b, s, d = x.shape
    assert (b, s, d) == (BATCH, SEQ, D_MODEL)
    tables = _rope_tables(s)
    x2 = x.reshape(b * s, d)
    per_layer = (attn_norm_g, conv_dw_w, conv_dw_b, conv_ln_g, conv_ln_b,
                 rnn_conv_w, rnn_conv_b, rglru_wa, rglru_ba, rglru_wx, rglru_bx, rglru_lambda,
                 cmp_pe, cmp_k_w1, cmp_k_w2, cmp_v_w1, cmp_v_w2, mlp_norm_g)
    mats = (w_conv_out, w_rnn_out, w_attn_out, w_o, w_mlp_up, w_mlp_down)
    w_in_bf_all = _permute_w_in_all(w_in)
    for l in range(DEPTH):
        x2 = _layer(x2, tables, w_in_bf_all, l, [_cast_bf16(m, l) for m in mats],
                    final_norm_g, *[p[l] for p in per_layer])
    return x2.reshape(b, s, d)
```

```python
import functools

import numpy as np
import jax
import jax.numpy as jnp
from jax import lax
from jax.experimental import pallas as pl
from jax.experimental.pallas import tpu as pltpu

F32 = jnp.float32
BF16 = jnp.bfloat16

D_MODEL = 2048
BATCH = 8
SEQ = 2048
DEPTH = 2

D_CONV = D_MODEL // 4
CONV_WIDTH = 31
D_RNN = 3 * D_MODEL // 8
RNN_BLOCKS = 6
RNN_BLOCK_W = D_RNN // RNN_BLOCKS
RNN_CONV_WIDTH = 4
RG_C = 8.0
N_Q_HEADS = 6
N_KV_HEADS = 2
HEAD_DIM = 128
GROUP = N_Q_HEADS // N_KV_HEADS
D_ATTN = N_Q_HEADS * HEAD_DIM
KV_W = N_KV_HEADS * HEAD_DIM
CMP_BLOCK = 32
CMP_STRIDE = 16
SEL_BLOCK = 64
SEL_TOP_N = 16
WINDOW = 512
ROPE_THETA = 10000.0
D_FF = 4 * D_MODEL
NORM_EPS = 1e-6
NEG_INF = -1e30
POS_INF = 1e30

N_CMP_PAD = SEQ // CMP_STRIDE
N_SEL = SEQ // SEL_BLOCK

LANES = 128
SLAB = 256
SL_GA, SL_GB, SL_GC = 0, 8, 16
SL_RX, SL_RG = 24, 27
SL_Q = 30
SL_KC, SL_VC, SL_KS = 33, 34, 35
SL_AV, SL_AG = 36, 38
SL_VS, SL_KW, SL_VW, SL_CG = 40, 41, 42, 43
N_SLABS = 44
N_IN_PAD = N_SLABS * SLAB

_IN_SIZES = (D_CONV, D_CONV, D_RNN, D_RNN, D_ATTN, KV_W, KV_W, KV_W, KV_W, KV_W, KV_W,
             3 * N_Q_HEADS, D_MODEL, D_MODEL, D_MODEL)
_IN_OFF = np.concatenate([[0], np.cumsum(_IN_SIZES)])
(_O_AV, _O_AG, _O_RX, _O_RG, _O_Q, _O_KC, _O_VC, _O_KS, _O_VS, _O_KW, _O_VW, _O_CG,
 _O_GA, _O_GB, _O_GC) = [int(v) for v in _IN_OFF[:-1]]
N_IN = int(_IN_OFF[-1])

VMEM_LIMIT = 56 * 1024 * 1024


def _cparams(sem, vmem=VMEM_LIMIT):
    return pltpu.CompilerParams(dimension_semantics=sem, vmem_limit_bytes=vmem)


def _sigmoid(x):
    return 0.5 * jnp.tanh(0.5 * x) + 0.5


def _gelu_tanh(x):
    c = np.float32(np.sqrt(2.0 / np.pi))
    return 0.5 * x * (1.0 + jnp.tanh(c * (x + 0.044715 * (x * x * x))))


IN_TM = 1024
IN_TN = 1024
NORM_RC = 128


def _rmsnorm_to(h_ref, x_ref, g_ref, rows):
    g = g_ref[...]

    def body(c, carry):
        r0 = pl.multiple_of(c * NORM_RC, NORM_RC)
        x = x_ref[pl.ds(r0, NORM_RC), :]
        ms = jnp.mean(x * x, axis=-1, keepdims=True)
        h_ref[pl.ds(r0, NORM_RC), :] = (x * lax.rsqrt(ms + NORM_EPS) * g).astype(h_ref.dtype)
        return carry

    lax.fori_loop(0, rows // NORM_RC, body, 0)


IN_CAST_STEPS = 8


def _inproj_kernel(x_ref, g_ref, w_ref, wu_ref, wd_ref, o_ref, wu_bf_ref, wd_bf_ref, h_ref):
    @pl.when(pl.program_id(1) == 0)
    def _():
        _rmsnorm_to(h_ref, x_ref, g_ref, IN_TM)

    @pl.when(pl.program_id(1) < IN_CAST_STEPS)
    def _():
        wu_bf_ref[...] = wu_ref[...].astype(wu_bf_ref.dtype)
        wd_bf_ref[...] = wd_ref[...].astype(wd_bf_ref.dtype)

    for k in range(IN_TN // SLAB):
        r = jnp.dot(h_ref[...], w_ref[:, k * SLAB:(k + 1) * SLAB], preferred_element_type=F32)
        o_ref[k] = r.astype(o_ref.dtype)


def _in_projection(x2, g, w_perm_all, w_up_all, w_down_all, layer):
    rows, d = x2.shape
    f = w_up_all.shape[2]
    n_i, n_j = rows // IN_TM, N_IN_PAD // IN_TN
    assert n_j >= IN_CAST_STEPS
    up_r, up_c = d // n_i, f // IN_CAST_STEPS
    dn_r = f // (n_i * IN_CAST_STEPS)
    piece = lambda j: jnp.minimum(j, IN_CAST_STEPS - 1)
    return pl.pallas_call(
        _inproj_kernel,
        grid=(n_i, n_j),
        in_specs=[pl.BlockSpec((IN_TM, d), lambda i, j: (i, 0)),
                  pl.BlockSpec((1, d), lambda i, j: (0, 0)),
                  pl.BlockSpec((None, d, IN_TN), lambda i, j: (layer, 0, j)),
                  pl.BlockSpec((None, up_r, up_c), lambda i, j: (layer, i, piece(j))),
                  pl.BlockSpec((None, dn_r, d), lambda i, j: (layer, i * IN_CAST_STEPS + piece(j), 0))],
        out_specs=[pl.BlockSpec((IN_TN // SLAB, IN_TM, SLAB), lambda i, j: (j, i, 0)),
                   pl.BlockSpec((up_r, up_c), lambda i, j: (i, piece(j))),
                   pl.BlockSpec((dn_r, d), lambda i, j: (i * IN_CAST_STEPS + piece(j), 0))],
        out_shape=[jax.ShapeDtypeStruct((N_SLABS, rows, SLAB), BF16),
                   jax.ShapeDtypeStruct((d, f), BF16),
                   jax.ShapeDtypeStruct((f, d), BF16)],
        scratch_shapes=[pltpu.VMEM((IN_TM, d), BF16)],
        compiler_params=_cparams(("parallel", "arbitrary")),
        name="in_projection",
    )(x2, g.reshape(1, d), w_perm_all, w_up_all, w_down_all)


MIX_TS = 256
MIX_TR = MIX_TS * BATCH
CONV_HALO = 256
CONV_RC = 64
CONV_PARTS = 4


def _conv_kernel(v_ref, g_ref, w_ref, b_ref, lg_ref, lb_ref, o_ref, ubuf, ybuf):
    nc = D_CONV // LANES
    per_slab = SLAB // LANES

    @pl.when(pl.program_id(0) == 0)
    def _():
        ubuf[:, 0:CONV_HALO, :] = jnp.zeros((nc, CONV_HALO, LANES), F32)

    def glu(b, carry):
        for c in range(nc):
            k, ls = c // per_slab, slice((c % per_slab) * LANES, (c % per_slab + 1) * LANES)
            v = v_ref[k, b, :, ls].astype(F32)
            g = g_ref[k, b, :, ls].astype(F32)
            ubuf[c, pl.ds(CONV_HALO + b, MIX_TS, stride=BATCH), :] = v * _sigmoid(g)
        return carry

    lax.fori_loop(0, BATCH, glu, 0)

    base = CONV_HALO - (CONV_WIDTH - 1) * BATCH

    def conv(i, carry):
        r0 = pl.multiple_of(i * CONV_RC, CONV_RC)
        rows = pl.ds(r0, CONV_RC)
        total = jnp.zeros((CONV_RC, 1), F32)
        for c in range(nc):
            ls = slice(c * LANES, (c + 1) * LANES)
            parts = [None] * CONV_PARTS
            for j in range(CONV_WIDTH):
                term = w_ref[j:j + 1, ls] * ubuf[c, pl.ds(r0 + base + BATCH * j, CONV_RC), :]
                k = j % CONV_PARTS
                parts[k] = term if parts[k] is None else parts[k] + term
            a = (parts[0] + parts[1]) + (parts[2] + parts[3]) + b_ref[:, ls]
            ybuf[c, rows, :] = a
            total = total + jnp.sum(a, axis=-1, keepdims=True)
        mu = total * (1.0 / D_CONV)
        sq = jnp.zeros((CONV_RC, 1), F32)
        for c in range(nc):
            cen = ybuf[c, rows, :] - mu
            sq = sq + jnp.sum(cen * cen, axis=-1, keepdims=True)
        inv = lax.rsqrt(sq * (1.0 / D_CONV) + NORM_EPS)
        for c in range(nc):
            ls = slice(c * LANES, (c + 1) * LANES)
            y = (ybuf[c, rows, :] - mu) * inv * lg_ref[:, ls] + lb_ref[:, ls]
            ybuf[c, rows, :] = y * _sigmoid(y)
        return carry

    lax.fori_loop(0, MIX_TR // CONV_RC, conv, 0)

    def put(b, carry):
        for c in range(nc):
            k, ls = c // per_slab, slice((c % per_slab) * LANES, (c % per_slab + 1) * LANES)
            o_ref[k, b, :, ls] = ybuf[c, pl.ds(b, MIX_TS, stride=BATCH), :].astype(o_ref.dtype)
        return carry

    lax.fori_loop(0, BATCH, put, 0)
    ubuf[:, 0:CONV_HALO, :] = ubuf[:, MIX_TR:MIX_TR + CONV_HALO, :]


def _conv_branch(proj, w, b, lg, lb):
    seq = proj.shape[1] // BATCH
    p4 = proj.reshape(N_SLABS, BATCH, seq, SLAB)
    nk = D_CONV // SLAB
    vec = lambda: pl.BlockSpec((1, D_CONV), lambda i: (0, 0))
    out = pl.pallas_call(
        _conv_kernel,
        grid=(seq // MIX_TS,),
        in_specs=[pl.BlockSpec((nk, BATCH, MIX_TS, SLAB), lambda i: (SL_AV // nk, 0, i, 0)),
                  pl.BlockSpec((nk, BATCH, MIX_TS, SLAB), lambda i: (SL_AG // nk, 0, i, 0)),
                  pl.BlockSpec((CONV_WIDTH, D_CONV), lambda i: (0, 0)),
                  vec(), vec(), vec()],
        out_specs=pl.BlockSpec((nk, BATCH, MIX_TS, SLAB), lambda i: (0, 0, i, 0)),
        out_shape=jax.ShapeDtypeStruct((nk, BATCH, seq, SLAB), BF16),
        scratch_shapes=[pltpu.VMEM((D_CONV // LANES, CONV_HALO + MIX_TR, LANES), F32),
                        pltpu.VMEM((D_CONV // LANES, MIX_TR, LANES), F32)],
        compiler_params=_cparams(("arbitrary",)),
        name="conv_branch",
    )(p4, p4, w, b.reshape(1, -1), lg.reshape(1, -1), lb.reshape(1, -1))
    return out.reshape(nk, BATCH * seq, SLAB)


RNN_HALO = 32
RNN_RC = 256


def _rglru_kernel(x_ref, gate_ref, cw_ref, cb_ref, wa_ref, ba_ref, wx_ref, bx_ref, lam_ref,
                  o_ref, xbuf, abuf, gbuf, hstate):
    per_slab = SLAB // RNN_BLOCK_W

    @pl.when(pl.program_id(0) == 0)
    def _():
        xbuf[:, 0:RNN_HALO, :] = jnp.zeros((RNN_BLOCKS, RNN_HALO, RNN_BLOCK_W), F32)
        hstate[...] = jnp.zeros_like(hstate)

    def slab_cols(n):
        return n // per_slab, slice((n % per_slab) * RNN_BLOCK_W, (n % per_slab + 1) * RNN_BLOCK_W)

    def load(b, carry):
        for n in range(RNN_BLOCKS):
            k, ls = slab_cols(n)
            xbuf[n, pl.ds(RNN_HALO + b, MIX_TS, stride=BATCH), :] = x_ref[k, b, :, ls].astype(F32)
        return carry

    lax.fori_loop(0, BATCH, load, 0)

    z = -lam_ref[...]
    softplus = jnp.maximum(z, 0.0) + jnp.log(1.0 + jnp.exp(-jnp.abs(z)))
    coef = -RG_C * softplus
    base = RNN_HALO - (RNN_CONV_WIDTH - 1) * BATCH

    def gates(i, carry):
        r0 = pl.multiple_of(i * RNN_RC, RNN_RC)
        for n in range(RNN_BLOCKS):
            cs = slice(n * RNN_BLOCK_W, (n + 1) * RNN_BLOCK_W)
            y = jnp.zeros((RNN_RC, RNN_BLOCK_W), F32)
            for j in range(RNN_CONV_WIDTH):
                y = y + cw_ref[j:j + 1, cs] * xbuf[n, pl.ds(r0 + base + BATCH * j, RNN_RC), :]
            y = y + cb_ref[:, cs]
            yb = y.astype(BF16)
            ra = _sigmoid(jnp.dot(yb, wa_ref[n], preferred_element_type=F32) + ba_ref[:, cs])
            ri = _sigmoid(jnp.dot(yb, wx_ref[n], preferred_element_type=F32) + bx_ref[:, cs])
            a = jnp.exp(coef[:, cs] * ra)
            abuf[n, pl.ds(r0, RNN_RC), :] = a
            gbuf[n, pl.ds(r0, RNN_RC), :] = jnp.sqrt(1.0 - a * a) * (ri * y)
        return carry

    lax.fori_loop(0, MIX_TR // RNN_RC, gates, 0)

    def step(t, h):
        r0 = pl.multiple_of(t * BATCH, BATCH)
        h = abuf[:, pl.ds(r0, BATCH), :] * h + gbuf[:, pl.ds(r0, BATCH), :]
        gbuf[:, pl.ds(r0, BATCH), :] = h
        return h

    hstate[...] = lax.fori_loop(0, MIX_TS, step, hstate[...], unroll=8)

    def put(b, carry):
        for n in range(RNN_BLOCKS):
            k, ls = slab_cols(n)
            h = gbuf[n, pl.ds(b, MIX_TS, stride=BATCH), :]
            o_ref[k, b, :, ls] = (h * _gelu_tanh(gate_ref[k, b, :, ls].astype(F32))).astype(o_ref.dtype)
        return carry

    lax.fori_loop(0, BATCH, put, 0)
    xbuf[:, 0:RNN_HALO, :] = xbuf[:, MIX_TR:MIX_TR + RNN_HALO, :]


def _rglru_branch(proj, cw, cb, wa, ba, wx, bx, lam):
    seq = proj.shape[1] // BATCH
    p4 = proj.reshape(N_SLABS, BATCH, seq, SLAB)
    nk = D_RNN // SLAB
    vec = lambda: pl.BlockSpec((1, D_RNN), lambda i: (0, 0))
    blk = lambda: pl.BlockSpec((RNN_BLOCKS, RNN_BLOCK_W, RNN_BLOCK_W), lambda i: (0, 0, 0))
    out = pl.pallas_call(
        _rglru_kernel,
        grid=(seq // MIX_TS,),
        in_specs=[pl.BlockSpec((nk, BATCH, MIX_TS, SLAB), lambda i: (SL_RX // nk, 0, i, 0)),
                  pl.BlockSpec((nk, BATCH, MIX_TS, SLAB), lambda i: (SL_RG // nk, 0, i, 0)),
                  pl.BlockSpec((RNN_CONV_WIDTH, D_RNN), lambda i: (0, 0)),
                  vec(), blk(), vec(), blk(), vec(), vec()],
        out_specs=pl.BlockSpec((nk, BATCH, MIX_TS, SLAB), lambda i: (0, 0, i, 0)),
        out_shape=jax.ShapeDtypeStruct((nk, BATCH, seq, SLAB), BF16),
        scratch_shapes=[pltpu.VMEM((RNN_BLOCKS, RNN_HALO + MIX_TR, RNN_BLOCK_W), F32),
                        pltpu.VMEM((RNN_BLOCKS, MIX_TR, RNN_BLOCK_W), F32),
                        pltpu.VMEM((RNN_BLOCKS, MIX_TR, RNN_BLOCK_W), F32),
                        pltpu.VMEM((RNN_BLOCKS, BATCH, RNN_BLOCK_W), F32)],
        compiler_params=_cparams(("arbitrary",)),
        name="rglru_branch",
    )(p4, p4, cw, cb.reshape(1, -1), wa.astype(BF16), ba.reshape(1, -1),
      wx.astype(BF16), bx.reshape(1, -1), lam.reshape(1, -1))
    return out.reshape(nk, BATCH * seq, SLAB)


def _compress_kernel(k_ref, v_ref, pe_ref, kw1_ref, vw1_ref, kw2_ref, vw2_ref, kc_ref, vc_ref, stage):
    half = CMP_STRIDE * HEAD_DIM
    pe = pe_ref[...].astype(BF16)
    for src, w1_ref, w2_ref, dst in ((k_ref, kw1_ref, kw2_ref, kc_ref),
                                     (v_ref, vw1_ref, vw2_ref, vc_ref)):
        w1 = w1_ref[...]
        pe_term = (jnp.dot(pe[:, :half], w1[:, :HEAD_DIM], preferred_element_type=F32)
                   + jnp.dot(pe[:, half:], w1[:, HEAD_DIM:], preferred_element_type=F32))
        for h in range(N_KV_HEADS):
            stage[...] = src[:, h * HEAD_DIM:(h + 1) * HEAD_DIM].astype(F32)
            x = jnp.concatenate([stage[pl.ds(l, N_CMP_PAD, stride=CMP_STRIDE), :].astype(BF16)
                                 for l in range(CMP_STRIDE)], axis=1)
            p = jnp.dot(x, w1, preferred_element_type=F32)
            hi_next = pltpu.roll(p[:, HEAD_DIM:], N_CMP_PAD - 1, 0)
            pre = p[:, :HEAD_DIM] + hi_next + pe_term[0:1, :]
            y = jnp.dot(_gelu_tanh(pre).astype(BF16), w2_ref[...], preferred_element_type=F32)
            dst[:, h * HEAD_DIM:(h + 1) * HEAD_DIM] = y.astype(dst.dtype)


def _compress(proj, pe, kw1, kw2, vw1, vw2):
    seq = proj.shape[1] // BATCH
    half = CMP_STRIDE * HEAD_DIM
    pe2 = jnp.zeros((8, CMP_BLOCK * HEAD_DIM), F32).at[0].set(pe.reshape(-1))

    def w1cat(w1):
        return jnp.concatenate([w1[:half], w1[half:]], axis=-1).astype(BF16)

    kv_spec = lambda sl: pl.BlockSpec((None, seq, SLAB), lambda b: (sl, b, 0))
    w1_spec = lambda: pl.BlockSpec((half, 2 * HEAD_DIM), lambda b: (0, 0))
    w2_spec = lambda: pl.BlockSpec((HEAD_DIM, HEAD_DIM), lambda b: (0, 0))
    out_spec = lambda: pl.BlockSpec((None, N_CMP_PAD, KV_W), lambda b: (b, 0, 0))
    return pl.pallas_call(
        _compress_kernel,
        grid=(BATCH,),
        in_specs=[kv_spec(SL_KC), kv_spec(SL_VC),
                  pl.BlockSpec((8, CMP_BLOCK * HEAD_DIM), lambda b: (0, 0)),
                  w1_spec(), w1_spec(), w2_spec(), w2_spec()],
        out_specs=[out_spec(), out_spec()],
        out_shape=[jax.ShapeDtypeStruct((BATCH, N_CMP_PAD, KV_W), BF16)] * 2,
        scratch_shapes=[pltpu.VMEM((seq, HEAD_DIM), F32)],
        compiler_params=_cparams(("parallel",)),
        name="compress_kv",
    )(proj, proj, pe2, w1cat(kw1), w1cat(vw1), kw2.astype(BF16), vw2.astype(BF16))


ATT_TQ = 256
ATT_TK = 256
ATT_NKB = SEQ // ATT_TK
ATT_WB = WINDOW // ATT_TK + 1
ATT_TS = 2 * ATT_TK
ATT_LANES = GROUP * ATT_TQ


def _tile_heads(x):
    return jnp.concatenate([x] * GROUP, axis=1)


def _attn_kernel(q_ref, ks_ref, vs_ref, kw_ref, vw_ref, kc_ref, vc_ref, cg_ref, cos_ref, sin_ref,
                 cost_ref, sint_ref, ovt_ref, o_ref, ksr, kwr, vst, vwt, vct, qaug, score_scr):
    qi = pl.program_id(1)
    scale = np.float32(HEAD_DIM ** -0.5 * np.log2(np.e))
    half = HEAD_DIM // 2

    @pl.when(qi == 0)
    def _():
        def prep(c, carry):
            r0 = pl.multiple_of(c * ATT_TK, ATT_TK)
            cos = cos_ref[pl.ds(r0, ATT_TK), :]
            sin = sin_ref[pl.ds(r0, ATT_TK), :]
            blk = (r0 + lax.broadcasted_iota(jnp.int32, (ATT_TK, HEAD_DIM), 0)) >> 6
            onehot = jnp.where(lax.broadcasted_iota(jnp.int32, (ATT_TK, HEAD_DIM), 1) == blk, 1.0, 0.0)
            for h in range(N_KV_HEADS):
                cs = slice(h * HEAD_DIM, (h + 1) * HEAD_DIM)
                x = ks_ref[pl.ds(r0, ATT_TK), cs].astype(F32)
                ksr[h, pl.ds(r0, ATT_TK), 0:HEAD_DIM] = (
                    x * cos + pltpu.roll(x, half, 1) * sin).astype(ksr.dtype)
                ksr[h, pl.ds(r0, ATT_TK), HEAD_DIM:2 * HEAD_DIM] = onehot.astype(ksr.dtype)
                x = kw_ref[pl.ds(r0, ATT_TK), cs].astype(F32)
                kwr[pl.ds(r0, ATT_TK), cs] = (x * cos + pltpu.roll(x, half, 1) * sin).astype(kwr.dtype)
                vwt[h, c] = vw_ref[pl.ds(r0, ATT_TK), cs].astype(F32).T.astype(vwt.dtype)
            return carry

        lax.fori_loop(0, ATT_NKB, prep, 0)

        def prep_vs(c, carry):
            for h in range(N_KV_HEADS):
                cs = slice(h * HEAD_DIM, (h + 1) * HEAD_DIM)
                parts = []
                for i in range(ATT_TS // ATT_TK):
                    r0 = pl.multiple_of(c * ATT_TS + i * ATT_TK, ATT_TK)
                    parts.append(vs_ref[pl.ds(r0, ATT_TK), cs].astype(F32).T.astype(vst.dtype))
                vst[h, c] = jnp.concatenate(parts, axis=1)
            return carry

        lax.fori_loop(0, SEQ // ATT_TS, prep_vs, 0)
        for h in range(N_KV_HEADS):
            vct[h] = vc_ref[:, h * HEAD_DIM:(h + 1) * HEAD_DIM].astype(F32).T.astype(vct.dtype)
        qaug[:, HEAD_DIM + N_SEL:, :] = jnp.zeros(
            (N_KV_HEADS, HEAD_DIM - N_SEL, ATT_LANES), qaug.dtype)

    t0 = qi * ATT_TQ
    cos_t = cost_ref[...]
    sin_t = sint_ref[...]
    gate_t = _sigmoid(cg_ref[:, 0:HEAD_DIM].astype(F32).T)

    t_c = t0 + (lax.broadcasted_iota(jnp.int32, (N_CMP_PAD, ATT_LANES), 1) & (ATT_TQ - 1))
    n_sub = lax.broadcasted_iota(jnp.int32, (N_CMP_PAD, ATT_LANES), 0)
    valid_c = (n_sub * CMP_STRIDE + (CMP_BLOCK - 1)) <= t_c
    t_s = t0 + lax.broadcasted_iota(jnp.int32, (N_SEL, ATT_TQ), 1)
    m_sub = lax.broadcasted_iota(jnp.int32, (N_SEL, ATT_TQ), 0)
    cur = t_s >> 6
    valid_s = m_sub <= cur
    forced = (m_sub == 0) | (m_sub == cur) | (m_sub == cur - 1)
    w_blk = jnp.maximum(qi - WINDOW // ATT_TK, 0)
    w_start = pl.multiple_of(w_blk * ATT_TK, ATT_TK)
    dist = ((t0 + lax.broadcasted_iota(jnp.int32, (ATT_WB * ATT_TK, ATT_TQ), 1))
            - (w_start + lax.broadcasted_iota(jnp.int32, (ATT_WB * ATT_TK, ATT_TQ), 0)))
    win_bias = jnp.where((dist >= 0) & (dist < WINDOW), 0.0, NEG_INF)

    o_cmp = []
    for h in range(N_KV_HEADS):
        hs = slice(h * HEAD_DIM, (h + 1) * HEAD_DIM)
        q_plain, q_rot = [], []
        for g in range(GROUP):
            hq = h * GROUP + g
            xt = q_ref[hq // 2, :, (hq % 2) * HEAD_DIM:(hq % 2 + 1) * HEAD_DIM].astype(F32).T
            swapped = jnp.concatenate([xt[half:], xt[:half]], axis=0)
            q_plain.append((xt * scale).astype(BF16))
            q_rot.append(((xt * cos_t + swapped * sin_t) * scale).astype(BF16))
        qn3 = jnp.concatenate(q_plain, axis=1)
        qaug[h, 0:HEAD_DIM, :] = jnp.concatenate(q_rot, axis=1)

        s = jnp.where(valid_c, jnp.dot(kc_ref[:, hs], qn3, preferred_element_type=F32), NEG_INF)
        mx = jnp.max(s, axis=0, keepdims=True)
        e = jnp.where(valid_c, jnp.exp2(s - mx), 0.0)
        den = jnp.sum(e, axis=0, keepdims=True)
        p = e / jnp.where(den > 0.0, den, 1.0)
        o_cmp.append(jnp.dot(vct[h], p.astype(BF16), preferred_element_type=F32))
        p_sum = p[:, 0:ATT_TQ]
        for g in range(1, GROUP):
            p_sum = p_sum + p[:, g * ATT_TQ:(g + 1) * ATT_TQ]

        imp = jnp.dot(ovt_ref[...], p_sum, preferred_element_type=F32,
                      precision=lax.Precision.HIGHEST)
        score = jnp.where(valid_s, jnp.where(forced, POS_INF, imp), NEG_INF)
        score_scr[...] = score
        m_v = lax.broadcasted_iota(jnp.int32, (8, ATT_TQ), 0)
        ranks = []
        for v in range(N_SEL // 8):
            sc_v = score_scr[8 * v:8 * v + 8, :]
            cnt = jnp.zeros((8, ATT_TQ), F32)
            for mp in range(N_SEL):
                row = score_scr[mp:mp + 1, :]
                if mp < 8 * v:
                    beats = row >= sc_v
                elif mp >= 8 * v + 8:
                    beats = row > sc_v
                else:
                    beats = (row > sc_v) | ((row == sc_v) & (m_v > mp - 8 * v))
                cnt = cnt + jnp.where(beats, 1.0, 0.0)
            keep = (cnt < SEL_TOP_N) & (sc_v > NEG_INF)
            ranks.append(jnp.where(keep, 0.0, NEG_INF))
        sel_bias = jnp.concatenate(ranks, axis=0)
        qaug[h, HEAD_DIM:HEAD_DIM + N_SEL, :] = _tile_heads(sel_bias).astype(qaug.dtype)

    def sel_step(j, carry, causal):
        k0 = pl.multiple_of(j * ATT_TS, ATT_TS)
        if causal:
            kpos = k0 + lax.broadcasted_iota(jnp.int32, (ATT_TS, ATT_TQ), 0)
            tpos = t0 + lax.broadcasted_iota(jnp.int32, (ATT_TS, ATT_TQ), 1)
            causal_bias = _tile_heads(jnp.where(kpos <= tpos, 0.0, NEG_INF))
        out = []
        for h in range(N_KV_HEADS):
            m_run, l_run, acc = carry[h]
            sc = jnp.dot(ksr[h, pl.ds(k0, ATT_TS), :], qaug[h], preferred_element_type=F32)
            if causal:
                sc = sc + causal_bias
            m_new = jnp.maximum(m_run, jnp.max(sc, axis=0, keepdims=True))
            alpha = jnp.exp2(m_run - m_new)
            pr = jnp.exp2(sc - m_new)
            l_new = alpha * l_run + jnp.sum(pr, axis=0, keepdims=True)
            acc = alpha * acc + jnp.dot(vst[h, j], pr.astype(BF16), preferred_element_type=F32)
            out.append((m_new, l_new, acc))
        return tuple(out)

    init = (jnp.full((1, ATT_LANES), NEG_INF, F32), jnp.zeros((1, ATT_LANES), F32),
            jnp.zeros((HEAD_DIM, ATT_LANES), F32))
    last = (t0 + ATT_TQ - 1) // ATT_TS
    carry = lax.fori_loop(0, last, lambda j, c: sel_step(j, c, False), (init,) * N_KV_HEADS)
    sel_out = sel_step(last, carry, True)

    for h in range(N_KV_HEADS):
        hs = slice(h * HEAD_DIM, (h + 1) * HEAD_DIM)
        _, l_s, acc_s = sel_out[h]
        o_sel = acc_s / l_s

        sw = (jnp.dot(kwr[pl.ds(w_start, ATT_WB * ATT_TK), hs], qaug[h, 0:HEAD_DIM, :],
                      preferred_element_type=F32) + _tile_heads(win_bias))
        mw = jnp.max(sw, axis=0, keepdims=True)
        pw = jnp.exp2(sw - mw)
        den_w = jnp.sum(pw, axis=0, keepdims=True)
        pw = pw.astype(BF16)
        o_win = None
        for i in range(ATT_WB):
            part = jnp.dot(vwt[h, w_blk + i], pw[i * ATT_TK:(i + 1) * ATT_TK],
                           preferred_element_type=F32)
            o_win = part if o_win is None else o_win + part
        o_win = o_win / den_w

        for g in range(GROUP):
            hq = h * GROUP + g
            ls = slice(g * ATT_TQ, (g + 1) * ATT_TQ)
            out_t = (gate_t[3 * hq:3 * hq + 1, :] * o_cmp[h][:, ls]
                     + gate_t[3 * hq + 1:3 * hq + 2, :] * o_sel[:, ls]
                     + gate_t[3 * hq + 2:3 * hq + 3, :] * o_win[:, ls])
            o_ref[hq // 2, :, (hq % 2) * HEAD_DIM:(hq % 2 + 1) * HEAD_DIM] = out_t.T.astype(o_ref.dtype)


def _attention(proj, kc, vc, tables):
    cos, sin, cos_t, sin_t = tables
    rows = proj.shape[1]
    seq = rows // BATCH
    nq = seq // ATT_TQ
    c_start = np.arange(N_CMP_PAD) * CMP_STRIDE
    s_start = np.arange(N_SEL) * SEL_BLOCK
    ovt = ((c_start[None, :] < s_start[:, None] + SEL_BLOCK)
           & (c_start[None, :] + CMP_BLOCK > s_start[:, None])
           & (np.arange(N_CMP_PAD)[None, :] < N_CMP_PAD - 1)).astype(np.float32)

    full = lambda sl: pl.BlockSpec((None, seq, SLAB), lambda b, qi: (sl, b, 0))
    cmp_spec = lambda: pl.BlockSpec((None, N_CMP_PAD, KV_W), lambda b, qi: (b, 0, 0))
    tab = lambda: pl.BlockSpec((seq, HEAD_DIM), lambda b, qi: (0, 0))
    tab_t = lambda: pl.BlockSpec((None, HEAD_DIM, ATT_TQ), lambda b, qi: (qi, 0, 0))
    nq3 = D_ATTN // SLAB
    kv_t = lambda tk: pltpu.VMEM((N_KV_HEADS, seq // tk, HEAD_DIM, tk), BF16)
    return pl.pallas_call(
        _attn_kernel,
        grid=(BATCH, nq),
        in_specs=[pl.BlockSpec((nq3, ATT_TQ, SLAB), lambda b, qi: (SL_Q // nq3, b * nq + qi, 0)),
                  full(SL_KS), full(SL_VS), full(SL_KW), full(SL_VW),
                  cmp_spec(), cmp_spec(),
                  pl.BlockSpec((None, ATT_TQ, SLAB), lambda b, qi: (SL_CG, b * nq + qi, 0)),
                  tab(), tab(), tab_t(), tab_t(),
                  pl.BlockSpec((N_SEL, N_CMP_PAD), lambda b, qi: (0, 0))],
        out_specs=pl.BlockSpec((nq3, ATT_TQ, SLAB), lambda b, qi: (0, b * nq + qi, 0)),
        out_shape=jax.ShapeDtypeStruct((nq3, rows, SLAB), BF16),
        scratch_shapes=[pltpu.VMEM((N_KV_HEADS, seq, 2 * HEAD_DIM), BF16),
                        pltpu.VMEM((seq, KV_W), BF16),
                        kv_t(ATT_TS), kv_t(ATT_TK),
                        pltpu.VMEM((N_KV_HEADS, HEAD_DIM, N_CMP_PAD), BF16),
                        pltpu.VMEM((N_KV_HEADS, 2 * HEAD_DIM, ATT_LANES), BF16),
                        pltpu.VMEM((N_SEL, ATT_TQ), F32)],
        compiler_params=_cparams(("parallel", "arbitrary")),
        name="sparse_attention",
    )(proj, proj, proj, proj, proj, kc, vc, proj, cos, sin, cos_t, sin_t, jnp.asarray(ovt))


MERGE_TM = 512
MERGE_TN = 512


def _cat_slabs(ref, first, n):
    return jnp.concatenate([ref[first + k] for k in range(n)], axis=-1)


def _merge_kernel(ua_ref, ub_ref, uc_ref, ga_ref, gb_ref, gc_ref, x_ref,
                  wa_ref, wb_ref, wc_ref, wo_ref, o_ref, y_ref):
    per = MERGE_TN // SLAB
    branches = ((ua_ref, ga_ref, wa_ref), (ub_ref, gb_ref, wb_ref), (uc_ref, gc_ref, wc_ref))
    acts = [_cat_slabs(u_ref, 0, u_ref.shape[0]) for u_ref, _, _ in branches]
    for c in range(D_MODEL // MERGE_TN):
        cs = slice(c * MERGE_TN, (c + 1) * MERGE_TN)
        y = None
        for u, (_, g_ref, w_ref) in zip(acts, branches):
            p = jnp.dot(u, w_ref[:, cs], preferred_element_type=F32)
            term = _sigmoid(_cat_slabs(g_ref, c * per, per).astype(F32)) * p
            y = term if y is None else y + term
        y_ref[:, cs] = y.astype(y_ref.dtype)
    o_ref[...] = x_ref[...] + jnp.dot(y_ref[...], wo_ref[...], preferred_element_type=F32)


def _merge(ua, ub, uc, proj, x2, wa, wb, wc, wo):
    rows, d = x2.shape
    ng = d // SLAB
    act = lambda n: pl.BlockSpec((n, MERGE_TM, SLAB), lambda i: (0, i, 0))
    gate = lambda sl: pl.BlockSpec((ng, MERGE_TM, SLAB), lambda i: (sl // ng, i, 0))
    res = lambda k: pl.BlockSpec((k, d), lambda i: (0, 0), pipeline_mode=pl.Buffered(1))
    return pl.pallas_call(
        _merge_kernel,
        grid=(rows // MERGE_TM,),
        in_specs=[act(ua.shape[0]), act(ub.shape[0]), act(uc.shape[0]),
                  gate(SL_GA), gate(SL_GB), gate(SL_GC),
                  pl.BlockSpec((MERGE_TM, d), lambda i: (i, 0)),
                  res(wa.shape[0]), res(wb.shape[0]), res(wc.shape[0]), res(wo.shape[0])],
        out_specs=pl.BlockSpec((MERGE_TM, d), lambda i: (i, 0)),
        out_shape=jax.ShapeDtypeStruct((rows, d), F32),
        scratch_shapes=[pltpu.VMEM((MERGE_TM, d), BF16)],
        compiler_params=_cparams(("parallel",)),
        name="merge_out_proj",
    )(ua, ub, uc, proj, proj, proj, x2, wa, wb, wc, wo)


MLP_TM = 1024
MLP_TF = 512


def _mlp_kernel(x_ref, g_ref, wu_ref, wd_ref, og_ref, o_ref, h_ref, *, norm_output):
    @pl.when(pl.program_id(1) == 0)
    def _():
        _rmsnorm_to(h_ref, x_ref, g_ref, MLP_TM)
        o_ref[...] = x_ref[...]

    a = jnp.dot(h_ref[...], wu_ref[...], preferred_element_type=F32)
    a = jnp.maximum(a, 0.0)
    o_ref[...] += jnp.dot((a * a).astype(BF16), wd_ref[...], preferred_element_type=F32)

    if norm_output:
        @pl.when(pl.program_id(1) == pl.num_programs(1) - 1)
        def _():
            _rmsnorm_to(o_ref, o_ref, og_ref, MLP_TM)


def _mlp(x2, g, wu, wd, out_g, norm_output):
    rows, d = x2.shape
    f = wu.shape[1]
    return pl.pallas_call(
        functools.partial(_mlp_kernel, norm_output=norm_output),
        grid=(rows // MLP_TM, f // MLP_TF),
        in_specs=[pl.BlockSpec((MLP_TM, d), lambda i, j: (i, 0)),
                  pl.BlockSpec((1, d), lambda i, j: (0, 0)),
                  pl.BlockSpec((d, MLP_TF), lambda i, j: (0, j)),
                  pl.BlockSpec((MLP_TF, d), lambda i, j: (j, 0)),
                  pl.BlockSpec((1, d), lambda i, j: (0, 0))],
        out_specs=pl.BlockSpec((MLP_TM, d), lambda i, j: (i, 0)),
        out_shape=jax.ShapeDtypeStruct((rows, d), F32),
        scratch_shapes=[pltpu.VMEM((MLP_TM, d), BF16)],
        compiler_params=_cparams(("parallel", "arbitrary")),
        name="mlp",
    )(x2, g.reshape(1, d), wu, wd, out_g.reshape(1, d))


def _prep_w_in_kernel(a_ref, b_ref, o_ref):
    j = pl.program_id(0)
    shift = _O_GA % SLAB
    n_gate = 3 * N_Q_HEADS
    n_layers = a_ref.shape[1]

    @pl.when(j < SL_RX)
    def _():
        for l in range(n_layers):
            at = a_ref[:, l, :].T
            bt = b_ref[:, l, :].T
            o_ref[l] = jnp.concatenate([at[:, shift:], bt[:, :shift]], axis=1).astype(o_ref.dtype)

    @pl.when((j >= SL_RX) & (j < SL_CG))
    def _():
        for l in range(n_layers):
            o_ref[l] = a_ref[:, l, :].T.astype(o_ref.dtype)

    @pl.when(j == SL_CG)
    def _():
        lane = lax.broadcasted_iota(jnp.int32, (D_MODEL, SLAB), 1)
        for l in range(n_layers):
            o_ref[l] = jnp.where(lane < n_gate, a_ref[:, l, :].T, 0.0).astype(o_ref.dtype)


def _permute_w_in_all(w_all):
    n_layers, d, n_in = w_all.shape
    w_t = jnp.transpose(w_all, (2, 0, 1))

    def src_block(j):
        blk = (_O_GA // SLAB) + j
        blk = jnp.where(j >= SL_RX, _O_RX // SLAB + (j - SL_RX), blk)
        blk = jnp.where(j >= SL_Q, _O_Q // SLAB + (j - SL_Q), blk)
        blk = jnp.where(j >= SL_AV, _O_AV // SLAB + (j - SL_AV), blk)
        blk = jnp.where(j >= SL_VS, _O_VS // SLAB + (j - SL_VS), blk)
        return jnp.where(j >= SL_CG, _O_CG // SLAB, blk)

    last_blk = (n_in - 1) // SLAB
    return pl.pallas_call(
        _prep_w_in_kernel,
        grid=(N_SLABS,),
        in_specs=[pl.BlockSpec((SLAB, n_layers, d), lambda j: (src_block(j), 0, 0)),
                  pl.BlockSpec((SLAB, n_layers, d),
                               lambda j: (jnp.where(j < SL_RX, src_block(j) + 1, last_blk), 0, 0))],
        out_specs=pl.BlockSpec((n_layers, d, SLAB), lambda j: (0, 0, j)),
        out_shape=jax.ShapeDtypeStruct((n_layers, d, N_IN_PAD), BF16),
        compiler_params=_cparams(("parallel",)),
        name="prep_w_in",
    )(w_t, w_t)


CAST_TR = 256


def _cast_kernel(w_ref, o_ref):
    o_ref[...] = w_ref[...].astype(o_ref.dtype)


def _cast_bf16(w_all, layer):
    _, r, c = w_all.shape
    tr = min(CAST_TR, r)
    return pl.pallas_call(
        _cast_kernel,
        grid=(r // tr,),
        in_specs=[pl.BlockSpec((None, tr, c), lambda i: (layer, i, 0))],
        out_specs=pl.BlockSpec((tr, c), lambda i: (i, 0)),
        out_shape=jax.ShapeDtypeStruct((r, c), BF16),
        compiler_params=_cparams(("parallel",)),
        name="cast_bf16",
    )(w_all)


def _rope_tables(s):
    inv = 1.0 / (ROPE_THETA ** (jnp.arange(0, HEAD_DIM, 2, dtype=F32) / HEAD_DIM))
    ang = jnp.arange(s, dtype=F32)[:, None] * inv[None, :]
    cos, sin = jnp.cos(ang), jnp.sin(ang)
    cos_f = jnp.concatenate([cos, cos], axis=-1)
    sin_f = jnp.concatenate([-sin, sin], axis=-1)
    tiles = lambda a: a.reshape(s // ATT_TQ, ATT_TQ, HEAD_DIM).transpose(0, 2, 1)
    return cos_f, sin_f, tiles(cos_f), tiles(sin_f)


def _layer(x2, tables, w_in_bf_all, w_mlp_up_all, w_mlp_down_all, layer, w_mats_bf, final_norm_g,
           attn_norm_g, conv_dw_w, conv_dw_b,
           conv_ln_g, conv_ln_b, rnn_conv_w, rnn_conv_b, rglru_wa, rglru_ba, rglru_wx, rglru_bx,
           rglru_lambda, cmp_pe, cmp_k_w1, cmp_k_w2, cmp_v_w1, cmp_v_w2, mlp_norm_g):
    w_conv_out, w_rnn_out, w_attn_out, w_o = w_mats_bf
    is_last = layer == DEPTH - 1
    proj, w_mlp_up, w_mlp_down = _in_projection(x2, attn_norm_g, w_in_bf_all, w_mlp_up_all,
                                                w_mlp_down_all, layer)
    ua = _conv_branch(proj, conv_dw_w, conv_dw_b, conv_ln_g, conv_ln_b)
    ub = _rglru_branch(proj, rnn_conv_w, rnn_conv_b, rglru_wa, rglru_ba, rglru_wx, rglru_bx,
                       rglru_lambda)
    kc, vc = _compress(proj, cmp_pe, cmp_k_w1, cmp_k_w2, cmp_v_w1, cmp_v_w2)
    uc = _attention(proj, kc, vc, tables)
    x2 = _merge(ua, ub, uc, proj, x2, w_conv_out, w_rnn_out, w_attn_out, w_o)
    return _mlp(x2, mlp_norm_g, w_mlp_up, w_mlp_down, final_norm_g, is_last)


def kernel(x, attn_norm_g, w_in, conv_dw_w, conv_dw_b, conv_ln_g, conv_ln_b, w_conv_out, rnn_conv_w, rnn_conv_b, rglru_wa, rglru_ba, rglru_wx, rglru_bx, rglru_lambda, w_rnn_out, cmp_pe, cmp_k_w1, cmp_k_w2, cmp_v_w1, cmp_v_w2, w_attn_out, w_o, mlp_norm_g, w_mlp_up, w_mlp_down, final_norm_g):
    b, s, d = x.shape
    assert (b, s, d) == (BATCH, SEQ, D_MODEL)
    tables = _rope_tables(s)
    x2 = x.reshape(b * s, d)
    per_layer = (attn_norm_g, conv_dw_w, conv_dw_b, conv_ln_g, conv_ln_b,
                 rnn_conv_w, rnn_conv_b, rglru_wa, rglru_ba, rglru_wx, rglru_bx, rglru_lambda,
                 cmp_pe, cmp_k_w1, cmp_k_w2, cmp_v_w1, cmp_v_w2, mlp_norm_g)
    mats = (w_conv_out, w_rnn_out, w_attn_out, w_o)
    w_in_bf_all = _permute_w_in_all(w_in)
    for l in range(DEPTH):
        x2 = _layer(x2, tables, w_in_bf_all, w_mlp_up, w_mlp_down, l, [_cast_bf16(m, l) for m in mats],
                    final_norm_g, *[p[l] for p in per_layer])
    return x2.reshape(b, s, d)
```

```python
import functools

import numpy as np
import jax
import jax.numpy as jnp
from jax import lax
from jax.experimental import pallas as pl
from jax.experimental.pallas import tpu as pltpu

F32 = jnp.float32
BF16 = jnp.bfloat16

D_MODEL = 2048
BATCH = 8
SEQ = 2048
DEPTH = 2

D_CONV = D_MODEL // 4
CONV_WIDTH = 31
D_RNN = 3 * D_MODEL // 8
RNN_BLOCKS = 6
RNN_BLOCK_W = D_RNN // RNN_BLOCKS
RNN_CONV_WIDTH = 4
RG_C = 8.0
N_Q_HEADS = 6
N_KV_HEADS = 2
HEAD_DIM = 128
GROUP = N_Q_HEADS // N_KV_HEADS
D_ATTN = N_Q_HEADS * HEAD_DIM
KV_W = N_KV_HEADS * HEAD_DIM
CMP_BLOCK = 32
CMP_STRIDE = 16
SEL_BLOCK = 64
SEL_TOP_N = 16
WINDOW = 512
ROPE_THETA = 10000.0
D_FF = 4 * D_MODEL
NORM_EPS = 1e-6
NEG_INF = -1e30
POS_INF = 1e30

N_CMP_PAD = SEQ // CMP_STRIDE
N_SEL = SEQ // SEL_BLOCK

LANES = 128
BF16_ROWS = 16
SLAB = 256
SL_GA, SL_GB, SL_GC = 0, 8, 16
SL_RX, SL_RG = 24, 27
SL_Q = 30
SL_KC, SL_VC, SL_KS = 33, 34, 35
SL_AV, SL_AG = 36, 38
SL_VS, SL_KW, SL_VW, SL_CG = 40, 41, 42, 43
N_SLABS = 44
N_IN_PAD = N_SLABS * SLAB

_IN_SIZES = (D_CONV, D_CONV, D_RNN, D_RNN, D_ATTN, KV_W, KV_W, KV_W, KV_W, KV_W, KV_W,
             3 * N_Q_HEADS, D_MODEL, D_MODEL, D_MODEL)
_IN_OFF = np.concatenate([[0], np.cumsum(_IN_SIZES)])
(_O_AV, _O_AG, _O_RX, _O_RG, _O_Q, _O_KC, _O_VC, _O_KS, _O_VS, _O_KW, _O_VW, _O_CG,
 _O_GA, _O_GB, _O_GC) = [int(v) for v in _IN_OFF[:-1]]
N_IN = int(_IN_OFF[-1])

VMEM_LIMIT = 62 * 1024 * 1024


def _cparams(sem, vmem=VMEM_LIMIT):
    return pltpu.CompilerParams(dimension_semantics=sem, vmem_limit_bytes=vmem)


def _sigmoid(x):
    return 0.5 * jnp.tanh(0.5 * x) + 0.5


def _gelu_tanh(x):
    c = np.float32(np.sqrt(2.0 / np.pi))
    return 0.5 * x * (1.0 + jnp.tanh(c * (x + 0.044715 * (x * x * x))))


IN_TM = 1024
IN_TN = N_IN_PAD // 4
NORM_RC = 128


def _rmsnorm_to(h_ref, x_ref, g_ref, rows):
    g = g_ref[...]

    def body(c, carry):
        r0 = pl.multiple_of(c * NORM_RC, NORM_RC)
        x = x_ref[pl.ds(r0, NORM_RC), :]
        ms = jnp.mean(x * x, axis=-1, keepdims=True)
        h_ref[pl.ds(r0, NORM_RC), :] = (x * lax.rsqrt(ms + NORM_EPS) * g).astype(h_ref.dtype)
        return carry

    lax.fori_loop(0, rows // NORM_RC, body, 0)


N_SIDE = 6


def _inproj_kernel(x_ref, g_ref, w_ref, *refs):
    side_in, o_ref, side_out, h_ref = refs[:N_SIDE], refs[N_SIDE], refs[N_SIDE + 1:-1], refs[-1]

    @pl.when(pl.program_id(1) == 0)
    def _():
        _rmsnorm_to(h_ref, x_ref, g_ref, IN_TM)

    for src, dst in zip(side_in, side_out):
        dst[...] = src[...].astype(dst.dtype)

    for k in range(IN_TN // SLAB):
        r = jnp.dot(h_ref[...], w_ref[:, k * SLAB:(k + 1) * SLAB], preferred_element_type=F32)
        o_ref[k] = r.astype(o_ref.dtype)


def _in_projection(x2, g, w_perm_all, side_weights, layer):
    rows, d = x2.shape
    n_i, n_j = rows // IN_TM, N_IN_PAD // IN_TN
    assert len(side_weights) == N_SIDE
    side_in, side_out, side_shapes = [], [], []
    for w in side_weights:
        _, r, c = w.shape
        pr = max(r // (n_i * n_j), BF16_ROWS)
        assert r % pr == 0 and pr % BF16_ROWS == 0
        piece = lambda i, j, last=r // pr - 1: jnp.minimum(i * n_j + j, last)
        side_in.append(pl.BlockSpec((None, pr, c), lambda i, j, piece=piece: (layer, piece(i, j), 0)))
        side_out.append(pl.BlockSpec((pr, c), lambda i, j, piece=piece: (piece(i, j), 0)))
        side_shapes.append(jax.ShapeDtypeStruct((r, c), BF16))
    out = pl.pallas_call(
        _inproj_kernel,
        grid=(n_i, n_j),
        in_specs=[pl.BlockSpec((IN_TM, d), lambda i, j: (i, 0)),
                  pl.BlockSpec((1, d), lambda i, j: (0, 0)),
                  pl.BlockSpec((None, d, IN_TN), lambda i, j: (layer, 0, j))] + side_in,
        out_specs=[pl.BlockSpec((IN_TN // SLAB, IN_TM, SLAB), lambda i, j: (j, i, 0))] + side_out,
        out_shape=[jax.ShapeDtypeStruct((N_SLABS, rows, SLAB), BF16)] + side_shapes,
        scratch_shapes=[pltpu.VMEM((IN_TM, d), BF16)],
        compiler_params=_cparams(("parallel", "arbitrary")),
        name="in_projection",
    )(x2, g.reshape(1, d), w_perm_all, *side_weights)
    return out[0], out[1:]


MIX_TS = 256
MIX_TR = MIX_TS * BATCH
CONV_HALO = 256
CONV_RC = 64
CONV_PARTS = 4


def _conv_kernel(v_ref, g_ref, w_ref, b_ref, lg_ref, lb_ref, o_ref, ubuf, ybuf):
    nc = D_CONV // LANES
    per_slab = SLAB // LANES

    @pl.when(pl.program_id(0) == 0)
    def _():
        ubuf[:, 0:CONV_HALO, :] = jnp.zeros((nc, CONV_HALO, LANES), F32)

    def glu(b, carry):
        for c in range(nc):
            k, ls = c // per_slab, slice((c % per_slab) * LANES, (c % per_slab + 1) * LANES)
            v = v_ref[k, b, :, ls].astype(F32)
            g = g_ref[k, b, :, ls].astype(F32)
            ubuf[c, pl.ds(CONV_HALO + b, MIX_TS, stride=BATCH), :] = v * _sigmoid(g)
        return carry

    lax.fori_loop(0, BATCH, glu, 0)

    base = CONV_HALO - (CONV_WIDTH - 1) * BATCH

    def conv(i, carry):
        r0 = pl.multiple_of(i * CONV_RC, CONV_RC)
        rows = pl.ds(r0, CONV_RC)
        total = jnp.zeros((CONV_RC, 1), F32)
        for c in range(nc):
            ls = slice(c * LANES, (c + 1) * LANES)
            parts = [None] * CONV_PARTS
            for j in range(CONV_WIDTH):
                term = w_ref[j:j + 1, ls] * ubuf[c, pl.ds(r0 + base + BATCH * j, CONV_RC), :]
                k = j % CONV_PARTS
                parts[k] = term if parts[k] is None else parts[k] + term
            a = (parts[0] + parts[1]) + (parts[2] + parts[3]) + b_ref[:, ls]
            ybuf[c, rows, :] = a
            total = total + jnp.sum(a, axis=-1, keepdims=True)
        mu = total * (1.0 / D_CONV)
        sq = jnp.zeros((CONV_RC, 1), F32)
        for c in range(nc):
            cen = ybuf[c, rows, :] - mu
            sq = sq + jnp.sum(cen * cen, axis=-1, keepdims=True)
        inv = lax.rsqrt(sq * (1.0 / D_CONV) + NORM_EPS)
        for c in range(nc):
            ls = slice(c * LANES, (c + 1) * LANES)
            y = (ybuf[c, rows, :] - mu) * inv * lg_ref[:, ls] + lb_ref[:, ls]
            ybuf[c, rows, :] = y * _sigmoid(y)
        return carry

    lax.fori_loop(0, MIX_TR // CONV_RC, conv, 0)

    def put(b, carry):
        for c in range(nc):
            k, ls = c // per_slab, slice((c % per_slab) * LANES, (c % per_slab + 1) * LANES)
            o_ref[k, b, :, ls] = ybuf[c, pl.ds(b, MIX_TS, stride=BATCH), :].astype(o_ref.dtype)
        return carry

    lax.fori_loop(0, BATCH, put, 0)
    ubuf[:, 0:CONV_HALO, :] = ubuf[:, MIX_TR:MIX_TR + CONV_HALO, :]


def _conv_branch(proj, w, b, lg, lb):
    seq = proj.shape[1] // BATCH
    p4 = proj.reshape(N_SLABS, BATCH, seq, SLAB)
    nk = D_CONV // SLAB
    vec = lambda: pl.BlockSpec((1, D_CONV), lambda i: (0, 0))
    out = pl.pallas_call(
        _conv_kernel,
        grid=(seq // MIX_TS,),
        in_specs=[pl.BlockSpec((nk, BATCH, MIX_TS, SLAB), lambda i: (SL_AV // nk, 0, i, 0)),
                  pl.BlockSpec((nk, BATCH, MIX_TS, SLAB), lambda i: (SL_AG // nk, 0, i, 0)),
                  pl.BlockSpec((CONV_WIDTH, D_CONV), lambda i: (0, 0)),
                  vec(), vec(), vec()],
        out_specs=pl.BlockSpec((nk, BATCH, MIX_TS, SLAB), lambda i: (0, 0, i, 0)),
        out_shape=jax.ShapeDtypeStruct((nk, BATCH, seq, SLAB), BF16),
        scratch_shapes=[pltpu.VMEM((D_CONV // LANES, CONV_HALO + MIX_TR, LANES), F32),
                        pltpu.VMEM((D_CONV // LANES, MIX_TR, LANES), F32)],
        compiler_params=_cparams(("arbitrary",)),
        name="conv_branch",
    )(p4, p4, w, b.reshape(1, -1), lg.reshape(1, -1), lb.reshape(1, -1))
    return out.reshape(nk, BATCH * seq, SLAB)


RNN_HALO = 32
RNN_RC = 256


def _rglru_kernel(x_ref, gate_ref, cw_ref, cb_ref, wa_ref, ba_ref, wx_ref, bx_ref, lam_ref,
                  o_ref, xbuf, abuf, gbuf, hstate):
    per_slab = SLAB // RNN_BLOCK_W

    @pl.when(pl.program_id(0) == 0)
    def _():
        xbuf[:, 0:RNN_HALO, :] = jnp.zeros((RNN_BLOCKS, RNN_HALO, RNN_BLOCK_W), F32)
        hstate[...] = jnp.zeros_like(hstate)

    def slab_cols(n):
        return n // per_slab, slice((n % per_slab) * RNN_BLOCK_W, (n % per_slab + 1) * RNN_BLOCK_W)

    def load(b, carry):
        for n in range(RNN_BLOCKS):
            k, ls = slab_cols(n)
            xbuf[n, pl.ds(RNN_HALO + b, MIX_TS, stride=BATCH), :] = x_ref[k, b, :, ls].astype(F32)
        return carry

    lax.fori_loop(0, BATCH, load, 0)

    z = -lam_ref[...]
    softplus = jnp.maximum(z, 0.0) + jnp.log(1.0 + jnp.exp(-jnp.abs(z)))
    coef = -RG_C * softplus
    base = RNN_HALO - (RNN_CONV_WIDTH - 1) * BATCH

    def gates(i, carry):
        r0 = pl.multiple_of(i * RNN_RC, RNN_RC)
        for n in range(RNN_BLOCKS):
            cs = slice(n * RNN_BLOCK_W, (n + 1) * RNN_BLOCK_W)
            y = jnp.zeros((RNN_RC, RNN_BLOCK_W), F32)
            for j in range(RNN_CONV_WIDTH):
                y = y + cw_ref[j:j + 1, cs] * xbuf[n, pl.ds(r0 + base + BATCH * j, RNN_RC), :]
            y = y + cb_ref[:, cs]
            yb = y.astype(BF16)
            ra = _sigmoid(jnp.dot(yb, wa_ref[n], preferred_element_type=F32) + ba_ref[:, cs])
            ri = _sigmoid(jnp.dot(yb, wx_ref[n], preferred_element_type=F32) + bx_ref[:, cs])
            a = jnp.exp(coef[:, cs] * ra)
            abuf[n, pl.ds(r0, RNN_RC), :] = a
            gbuf[n, pl.ds(r0, RNN_RC), :] = jnp.sqrt(1.0 - a * a) * (ri * y)
        return carry

    lax.fori_loop(0, MIX_TR // RNN_RC, gates, 0)

    def step(t, h):
        r0 = pl.multiple_of(t * BATCH, BATCH)
        h = abuf[:, pl.ds(r0, BATCH), :] * h + gbuf[:, pl.ds(r0, BATCH), :]
        gbuf[:, pl.ds(r0, BATCH), :] = h
        return h

    hstate[...] = lax.fori_loop(0, MIX_TS, step, hstate[...], unroll=8)

    def put(b, carry):
        for n in range(RNN_BLOCKS):
            k, ls = slab_cols(n)
            h = gbuf[n, pl.ds(b, MIX_TS, stride=BATCH), :]
            o_ref[k, b, :, ls] = (h * _gelu_tanh(gate_ref[k, b, :, ls].astype(F32))).astype(o_ref.dtype)
        return carry

    lax.fori_loop(0, BATCH, put, 0)
    xbuf[:, 0:RNN_HALO, :] = xbuf[:, MIX_TR:MIX_TR + RNN_HALO, :]


def _rglru_branch(proj, cw, cb, wa, ba, wx, bx, lam):
    seq = proj.shape[1] // BATCH
    p4 = proj.reshape(N_SLABS, BATCH, seq, SLAB)
    nk = D_RNN // SLAB
    vec = lambda: pl.BlockSpec((1, D_RNN), lambda i: (0, 0))
    blk = lambda: pl.BlockSpec((RNN_BLOCKS, RNN_BLOCK_W, RNN_BLOCK_W), lambda i: (0, 0, 0))
    out = pl.pallas_call(
        _rglru_kernel,
        grid=(seq // MIX_TS,),
        in_specs=[pl.BlockSpec((nk, BATCH, MIX_TS, SLAB), lambda i: (SL_RX // nk, 0, i, 0)),
                  pl.BlockSpec((nk, BATCH, MIX_TS, SLAB), lambda i: (SL_RG // nk, 0, i, 0)),
                  pl.BlockSpec((RNN_CONV_WIDTH, D_RNN), lambda i: (0, 0)),
                  vec(), blk(), vec(), blk(), vec(), vec()],
        out_specs=pl.BlockSpec((nk, BATCH, MIX_TS, SLAB), lambda i: (0, 0, i, 0)),
        out_shape=jax.ShapeDtypeStruct((nk, BATCH, seq, SLAB), BF16),
        scratch_shapes=[pltpu.VMEM((RNN_BLOCKS, RNN_HALO + MIX_TR, RNN_BLOCK_W), F32),
                        pltpu.VMEM((RNN_BLOCKS, MIX_TR, RNN_BLOCK_W), F32),
                        pltpu.VMEM((RNN_BLOCKS, MIX_TR, RNN_BLOCK_W), F32),
                        pltpu.VMEM((RNN_BLOCKS, BATCH, RNN_BLOCK_W), F32)],
        compiler_params=_cparams(("arbitrary",)),
        name="rglru_branch",
    )(p4, p4, cw, cb.reshape(1, -1), wa.astype(BF16), ba.reshape(1, -1),
      wx.astype(BF16), bx.reshape(1, -1), lam.reshape(1, -1))
    return out.reshape(nk, BATCH * seq, SLAB)


def _compress_kernel(k_ref, v_ref, pe_ref, kw1_ref, vw1_ref, kw2_ref, vw2_ref, kc_ref, vc_ref, stage):
    half = CMP_STRIDE * HEAD_DIM
    pe = pe_ref[...].astype(BF16)
    for src, w1_ref, w2_ref, dst in ((k_ref, kw1_ref, kw2_ref, kc_ref),
                                     (v_ref, vw1_ref, vw2_ref, vc_ref)):
        w1 = w1_ref[...]
        pe_term = (jnp.dot(pe[:, :half], w1[:, :HEAD_DIM], preferred_element_type=F32)
                   + jnp.dot(pe[:, half:], w1[:, HEAD_DIM:], preferred_element_type=F32))
        for h in range(N_KV_HEADS):
            stage[...] = src[:, h * HEAD_DIM:(h + 1) * HEAD_DIM].astype(F32)
            x = jnp.concatenate([stage[pl.ds(l, N_CMP_PAD, stride=CMP_STRIDE), :].astype(BF16)
                                 for l in range(CMP_STRIDE)], axis=1)
            p = jnp.dot(x, w1, preferred_element_type=F32)
            hi_next = pltpu.roll(p[:, HEAD_DIM:], N_CMP_PAD - 1, 0)
            pre = p[:, :HEAD_DIM] + hi_next + pe_term[0:1, :]
            y = jnp.dot(_gelu_tanh(pre).astype(BF16), w2_ref[...], preferred_element_type=F32)
            dst[:, h * HEAD_DIM:(h + 1) * HEAD_DIM] = y.astype(dst.dtype)


def _compress(proj, pe, kw1, kw2, vw1, vw2):
    seq = proj.shape[1] // BATCH
    half = CMP_STRIDE * HEAD_DIM
    pe2 = jnp.zeros((8, CMP_BLOCK * HEAD_DIM), F32).at[0].set(pe.reshape(-1))

    def w1cat(w1):
        return jnp.concatenate([w1[:half], w1[half:]], axis=-1).astype(BF16)

    kv_spec = lambda sl: pl.BlockSpec((None, seq, SLAB), lambda b: (sl, b, 0))
    w1_spec = lambda: pl.BlockSpec((half, 2 * HEAD_DIM), lambda b: (0, 0))
    w2_spec = lambda: pl.BlockSpec((HEAD_DIM, HEAD_DIM), lambda b: (0, 0))
    out_spec = lambda: pl.BlockSpec((None, N_CMP_PAD, KV_W), lambda b: (b, 0, 0))
    return pl.pallas_call(
        _compress_kernel,
        grid=(BATCH,),
        in_specs=[kv_spec(SL_KC), kv_spec(SL_VC),
                  pl.BlockSpec((8, CMP_BLOCK * HEAD_DIM), lambda b: (0, 0)),
                  w1_spec(), w1_spec(), w2_spec(), w2_spec()],
        out_specs=[out_spec(), out_spec()],
        out_shape=[jax.ShapeDtypeStruct((BATCH, N_CMP_PAD, KV_W), BF16)] * 2,
        scratch_shapes=[pltpu.VMEM((seq, HEAD_DIM), F32)],
        compiler_params=_cparams(("parallel",)),
        name="compress_kv",
    )(proj, proj, pe2, w1cat(kw1), w1cat(vw1), kw2.astype(BF16), vw2.astype(BF16))


ATT_TQ = 256
ATT_TK = 256
ATT_NKB = SEQ // ATT_TK
ATT_WB = WINDOW // ATT_TK + 1
ATT_TS = 2 * ATT_TK
ATT_LANES = GROUP * ATT_TQ


def _tile_heads(x):
    return jnp.concatenate([x] * GROUP, axis=1)


def _attn_kernel(q_ref, ks_ref, vs_ref, kw_ref, vw_ref, kc_ref, vc_ref, cg_ref, cos_ref, sin_ref,
                 cost_ref, sint_ref, ovt_ref, o_ref, ksr, kwr, vst, vwt, vct, qaug, score_scr):
    qi = pl.program_id(1)
    scale = np.float32(HEAD_DIM ** -0.5 * np.log2(np.e))
    half = HEAD_DIM // 2

    @pl.when(qi == 0)
    def _():
        def prep(c, carry):
            r0 = pl.multiple_of(c * ATT_TK, ATT_TK)
            cos = cos_ref[pl.ds(r0, ATT_TK), :]
            sin = sin_ref[pl.ds(r0, ATT_TK), :]
            blk = (r0 + lax.broadcasted_iota(jnp.int32, (ATT_TK, HEAD_DIM), 0)) >> 6
            onehot = jnp.where(lax.broadcasted_iota(jnp.int32, (ATT_TK, HEAD_DIM), 1) == blk, 1.0, 0.0)
            for h in range(N_KV_HEADS):
                cs = slice(h * HEAD_DIM, (h + 1) * HEAD_DIM)
                x = ks_ref[pl.ds(r0, ATT_TK), cs].astype(F32)
                ksr[h, pl.ds(r0, ATT_TK), 0:HEAD_DIM] = (
                    x * cos + pltpu.roll(x, half, 1) * sin).astype(ksr.dtype)
                ksr[h, pl.ds(r0, ATT_TK), HEAD_DIM:2 * HEAD_DIM] = onehot.astype(ksr.dtype)
                x = kw_ref[pl.ds(r0, ATT_TK), cs].astype(F32)
                kwr[pl.ds(r0, ATT_TK), cs] = (x * cos + pltpu.roll(x, half, 1) * sin).astype(kwr.dtype)
                vwt[h, c] = vw_ref[pl.ds(r0, ATT_TK), cs].astype(F32).T.astype(vwt.dtype)
            return carry

        lax.fori_loop(0, ATT_NKB, prep, 0)

        def prep_vs(c, carry):
            for h in range(N_KV_HEADS):
                cs = slice(h * HEAD_DIM, (h + 1) * HEAD_DIM)
                parts = []
                for i in range(ATT_TS // ATT_TK):
                    r0 = pl.multiple_of(c * ATT_TS + i * ATT_TK, ATT_TK)
                    parts.append(vs_ref[pl.ds(r0, ATT_TK), cs].astype(F32).T.astype(vst.dtype))
                vst[h, c] = jnp.concatenate(parts, axis=1)
            return carry

        lax.fori_loop(0, SEQ // ATT_TS, prep_vs, 0)
        for h in range(N_KV_HEADS):
            vct[h] = vc_ref[:, h * HEAD_DIM:(h + 1) * HEAD_DIM].astype(F32).T.astype(vct.dtype)
        qaug[:, HEAD_DIM + N_SEL:, :] = jnp.zeros(
            (N_KV_HEADS, HEAD_DIM - N_SEL, ATT_LANES), qaug.dtype)

    t0 = qi * ATT_TQ
    cos_t = cost_ref[...]
    sin_t = sint_ref[...]
    gate_t = _sigmoid(cg_ref[:, 0:HEAD_DIM].astype(F32).T)

    t_c = t0 + (lax.broadcasted_iota(jnp.int32, (N_CMP_PAD, ATT_LANES), 1) & (ATT_TQ - 1))
    n_sub = lax.broadcasted_iota(jnp.int32, (N_CMP_PAD, ATT_LANES), 0)
    valid_c = (n_sub * CMP_STRIDE + (CMP_BLOCK - 1)) <= t_c
    t_s = t0 + lax.broadcasted_iota(jnp.int32, (N_SEL, ATT_TQ), 1)
    m_sub = lax.broadcasted_iota(jnp.int32, (N_SEL, ATT_TQ), 0)
    cur = t_s >> 6
    valid_s = m_sub <= cur
    forced = (m_sub == 0) | (m_sub == cur) | (m_sub == cur - 1)
    w_blk = jnp.maximum(qi - WINDOW // ATT_TK, 0)
    w_start = pl.multiple_of(w_blk * ATT_TK, ATT_TK)
    dist = ((t0 + lax.broadcasted_iota(jnp.int32, (ATT_WB * ATT_TK, ATT_TQ), 1))
            - (w_start + lax.broadcasted_iota(jnp.int32, (ATT_WB * ATT_TK, ATT_TQ), 0)))
    win_bias = jnp.where((dist >= 0) & (dist < WINDOW), 0.0, NEG_INF)

    o_cmp = []
    for h in range(N_KV_HEADS):
        hs = slice(h * HEAD_DIM, (h + 1) * HEAD_DIM)
        q_plain, q_rot = [], []
        for g in range(GROUP):
            hq = h * GROUP + g
            xt = q_ref[hq // 2, :, (hq % 2) * HEAD_DIM:(hq % 2 + 1) * HEAD_DIM].astype(F32).T
            swapped = jnp.concatenate([xt[half:], xt[:half]], axis=0)
            q_plain.append((xt * scale).astype(BF16))
            q_rot.append(((xt * cos_t + swapped * sin_t) * scale).astype(BF16))
        qn3 = jnp.concatenate(q_plain, axis=1)
        qaug[h, 0:HEAD_DIM, :] = jnp.concatenate(q_rot, axis=1)

        s = jnp.where(valid_c, jnp.dot(kc_ref[:, hs], qn3, preferred_element_type=F32), NEG_INF)
        mx = jnp.max(s, axis=0, keepdims=True)
        e = jnp.where(valid_c, jnp.exp2(s - mx), 0.0)
        den = jnp.sum(e, axis=0, keepdims=True)
        p = e / jnp.where(den > 0.0, den, 1.0)
        o_cmp.append(jnp.dot(vct[h], p.astype(BF16), preferred_element_type=F32))
        p_sum = p[:, 0:ATT_TQ]
        for g in range(1, GROUP):
            p_sum = p_sum + p[:, g * ATT_TQ:(g + 1) * ATT_TQ]

        imp = jnp.dot(ovt_ref[...], p_sum, preferred_element_type=F32,
                      precision=lax.Precision.HIGHEST)
        score = jnp.where(valid_s, jnp.where(forced, POS_INF, imp), NEG_INF)
        score_scr[...] = score
        m_v = lax.broadcasted_iota(jnp.int32, (8, ATT_TQ), 0)
        ranks = []
        for v in range(N_SEL // 8):
            sc_v = score_scr[8 * v:8 * v + 8, :]
            cnt = jnp.zeros((8, ATT_TQ), F32)
            for mp in range(N_SEL):
                row = score_scr[mp:mp + 1, :]
                if mp < 8 * v:
                    beats = row >= sc_v
                elif mp >= 8 * v + 8:
                    beats = row > sc_v
                else:
                    beats = (row > sc_v) | ((row == sc_v) & (m_v > mp - 8 * v))
                cnt = cnt + jnp.where(beats, 1.0, 0.0)
            keep = (cnt < SEL_TOP_N) & (sc_v > NEG_INF)
            ranks.append(jnp.where(keep, 0.0, NEG_INF))
        sel_bias = jnp.concatenate(ranks, axis=0)
        qaug[h, HEAD_DIM:HEAD_DIM + N_SEL, :] = _tile_heads(sel_bias).astype(qaug.dtype)

    def sel_step(j, carry, causal):
        k0 = pl.multiple_of(j * ATT_TS, ATT_TS)
        if causal:
            kpos = k0 + lax.broadcasted_iota(jnp.int32, (ATT_TS, ATT_TQ), 0)
            tpos = t0 + lax.broadcasted_iota(jnp.int32, (ATT_TS, ATT_TQ), 1)
            causal_bias = _tile_heads(jnp.where(kpos <= tpos, 0.0, NEG_INF))
        out = []
        for h in range(N_KV_HEADS):
            m_run, l_run, acc = carry[h]
            sc = jnp.dot(ksr[h, pl.ds(k0, ATT_TS), :], qaug[h], preferred_element_type=F32)
            if causal:
                sc = sc + causal_bias
            m_new = jnp.maximum(m_run, jnp.max(sc, axis=0, keepdims=True))
            alpha = jnp.exp2(m_run - m_new)
            pr = jnp.exp2(sc - m_new)
            l_new = alpha * l_run + jnp.sum(pr, axis=0, keepdims=True)
            acc = alpha * acc + jnp.dot(vst[h, j], pr.astype(BF16), preferred_element_type=F32)
            out.append((m_new, l_new, acc))
        return tuple(out)

    init = (jnp.full((1, ATT_LANES), NEG_INF, F32), jnp.zeros((1, ATT_LANES), F32),
            jnp.zeros((HEAD_DIM, ATT_LANES), F32))
    last = (t0 + ATT_TQ - 1) // ATT_TS
    carry = lax.fori_loop(0, last, lambda j, c: sel_step(j, c, False), (init,) * N_KV_HEADS)
    sel_out = sel_step(last, carry, True)

    for h in range(N_KV_HEADS):
        hs = slice(h * HEAD_DIM, (h + 1) * HEAD_DIM)
        _, l_s, acc_s = sel_out[h]
        o_sel = acc_s / l_s

        sw = (jnp.dot(kwr[pl.ds(w_start, ATT_WB * ATT_TK), hs], qaug[h, 0:HEAD_DIM, :],
                      preferred_element_type=F32) + _tile_heads(win_bias))
        mw = jnp.max(sw, axis=0, keepdims=True)
        pw = jnp.exp2(sw - mw)
        den_w = jnp.sum(pw, axis=0, keepdims=True)
        pw = pw.astype(BF16)
        o_win = None
        for i in range(ATT_WB):
            part = jnp.dot(vwt[h, w_blk + i], pw[i * ATT_TK:(i + 1) * ATT_TK],
                           preferred_element_type=F32)
            o_win = part if o_win is None else o_win + part
        o_win = o_win / den_w

        for g in range(GROUP):
            hq = h * GROUP + g
            ls = slice(g * ATT_TQ, (g + 1) * ATT_TQ)
            out_t = (gate_t[3 * hq:3 * hq + 1, :] * o_cmp[h][:, ls]
                     + gate_t[3 * hq + 1:3 * hq + 2, :] * o_sel[:, ls]
                     + gate_t[3 * hq + 2:3 * hq + 3, :] * o_win[:, ls])
            o_ref[hq // 2, :, (hq % 2) * HEAD_DIM:(hq % 2 + 1) * HEAD_DIM] = out_t.T.astype(o_ref.dtype)


def _attention(proj, kc, vc, tables):
    cos, sin, cos_t, sin_t = tables
    rows = proj.shape[1]
    seq = rows // BATCH
    nq = seq // ATT_TQ
    c_start = np.arange(N_CMP_PAD) * CMP_STRIDE
    s_start = np.arange(N_SEL) * SEL_BLOCK
    ovt = ((c_start[None, :] < s_start[:, None] + SEL_BLOCK)
           & (c_start[None, :] + CMP_BLOCK > s_start[:, None])
           & (np.arange(N_CMP_PAD)[None, :] < N_CMP_PAD - 1)).astype(np.float32)

    full = lambda sl: pl.BlockSpec((None, seq, SLAB), lambda b, qi: (sl, b, 0))
    cmp_spec = lambda: pl.BlockSpec((None, N_CMP_PAD, KV_W), lambda b, qi: (b, 0, 0))
    tab = lambda: pl.BlockSpec((seq, HEAD_DIM), lambda b, qi: (0, 0))
    tab_t = lambda: pl.BlockSpec((None, HEAD_DIM, ATT_TQ), lambda b, qi: (qi, 0, 0))
    nq3 = D_ATTN // SLAB
    kv_t = lambda tk: pltpu.VMEM((N_KV_HEADS, seq // tk, HEAD_DIM, tk), BF16)
    return pl.pallas_call(
        _attn_kernel,
        grid=(BATCH, nq),
        in_specs=[pl.BlockSpec((nq3, ATT_TQ, SLAB), lambda b, qi: (SL_Q // nq3, b * nq + qi, 0)),
                  full(SL_KS), full(SL_VS), full(SL_KW), full(SL_VW),
                  cmp_spec(), cmp_spec(),
                  pl.BlockSpec((None, ATT_TQ, SLAB), lambda b, qi: (SL_CG, b * nq + qi, 0)),
                  tab(), tab(), tab_t(), tab_t(),
                  pl.BlockSpec((N_SEL, N_CMP_PAD), lambda b, qi: (0, 0))],
        out_specs=pl.BlockSpec((nq3, ATT_TQ, SLAB), lambda b, qi: (0, b * nq + qi, 0)),
        out_shape=jax.ShapeDtypeStruct((nq3, rows, SLAB), BF16),
        scratch_shapes=[pltpu.VMEM((N_KV_HEADS, seq, 2 * HEAD_DIM), BF16),
                        pltpu.VMEM((seq, KV_W), BF16),
                        kv_t(ATT_TS), kv_t(ATT_TK),
                        pltpu.VMEM((N_KV_HEADS, HEAD_DIM, N_CMP_PAD), BF16),
                        pltpu.VMEM((N_KV_HEADS, 2 * HEAD_DIM, ATT_LANES), BF16),
                        pltpu.VMEM((N_SEL, ATT_TQ), F32)],
        compiler_params=_cparams(("parallel", "arbitrary")),
        name="sparse_attention",
    )(proj, proj, proj, proj, proj, kc, vc, proj, cos, sin, cos_t, sin_t, jnp.asarray(ovt))


MERGE_TM = 512
MERGE_TN = 512


def _cat_slabs(ref, first, n):
    return jnp.concatenate([ref[first + k] for k in range(n)], axis=-1)


def _merge_kernel(ua_ref, ub_ref, uc_ref, ga_ref, gb_ref, gc_ref, x_ref,
                  wa_ref, wb_ref, wc_ref, wo_ref, o_ref, y_ref):
    per = MERGE_TN // SLAB
    branches = ((ua_ref, ga_ref, wa_ref), (ub_ref, gb_ref, wb_ref), (uc_ref, gc_ref, wc_ref))
    acts = [_cat_slabs(u_ref, 0, u_ref.shape[0]) for u_ref, _, _ in branches]
    for c in range(D_MODEL // MERGE_TN):
        cs = slice(c * MERGE_TN, (c + 1) * MERGE_TN)
        y = None
        for u, (_, g_ref, w_ref) in zip(acts, branches):
            p = jnp.dot(u, w_ref[:, cs], preferred_element_type=F32)
            term = _sigmoid(_cat_slabs(g_ref, c * per, per).astype(F32)) * p
            y = term if y is None else y + term
        y_ref[:, cs] = y.astype(y_ref.dtype)
    o_ref[...] = x_ref[...] + jnp.dot(y_ref[...], wo_ref[...], preferred_element_type=F32)


def _merge(ua, ub, uc, proj, x2, wa, wb, wc, wo):
    rows, d = x2.shape
    ng = d // SLAB
    act = lambda n: pl.BlockSpec((n, MERGE_TM, SLAB), lambda i: (0, i, 0))
    gate = lambda sl: pl.BlockSpec((ng, MERGE_TM, SLAB), lambda i: (sl // ng, i, 0))
    res = lambda k: pl.BlockSpec((k, d), lambda i: (0, 0), pipeline_mode=pl.Buffered(1))
    return pl.pallas_call(
        _merge_kernel,
        grid=(rows // MERGE_TM,),
        in_specs=[act(ua.shape[0]), act(ub.shape[0]), act(uc.shape[0]),
                  gate(SL_GA), gate(SL_GB), gate(SL_GC),
                  pl.BlockSpec((MERGE_TM, d), lambda i: (i, 0)),
                  res(wa.shape[0]), res(wb.shape[0]), res(wc.shape[0]), res(wo.shape[0])],
        out_specs=pl.BlockSpec((MERGE_TM, d), lambda i: (i, 0)),
        out_shape=jax.ShapeDtypeStruct((rows, d), F32),
        scratch_shapes=[pltpu.VMEM((MERGE_TM, d), BF16)],
        compiler_params=_cparams(("parallel",)),
        name="merge_out_proj",
    )(ua, ub, uc, proj, proj, proj, x2, wa, wb, wc, wo)


MLP_TM = 1024
MLP_TF = 1024


def _mlp_kernel(x_ref, g_ref, wu_ref, wd_ref, og_ref, o_ref, h_ref, *, norm_output):
    @pl.when(pl.program_id(1) == 0)
    def _():
        _rmsnorm_to(h_ref, x_ref, g_ref, MLP_TM)
        o_ref[...] = x_ref[...]

    a = jnp.dot(h_ref[...], wu_ref[...], preferred_element_type=F32)
    a = jnp.maximum(a, 0.0)
    o_ref[...] += jnp.dot((a * a).astype(BF16), wd_ref[...], preferred_element_type=F32)

    if norm_output:
        @pl.when(pl.program_id(1) == pl.num_programs(1) - 1)
        def _():
            _rmsnorm_to(o_ref, o_ref, og_ref, MLP_TM)


def _mlp(x2, g, wu, wd, out_g, norm_output):
    rows, d = x2.shape
    f = wu.shape[1]
    return pl.pallas_call(
        functools.partial(_mlp_kernel, norm_output=norm_output),
        grid=(rows // MLP_TM, f // MLP_TF),
        in_specs=[pl.BlockSpec((MLP_TM, d), lambda i, j: (i, 0)),
                  pl.BlockSpec((1, d), lambda i, j: (0, 0)),
                  pl.BlockSpec((d, MLP_TF), lambda i, j: (0, j)),
                  pl.BlockSpec((MLP_TF, d), lambda i, j: (j, 0)),
                  pl.BlockSpec((1, d), lambda i, j: (0, 0))],
        out_specs=pl.BlockSpec((MLP_TM, d), lambda i, j: (i, 0)),
        out_shape=jax.ShapeDtypeStruct((rows, d), F32),
        scratch_shapes=[pltpu.VMEM((MLP_TM, d), BF16)],
        compiler_params=_cparams(("parallel", "arbitrary")),
        name="mlp",
    )(x2, g.reshape(1, d), wu, wd, out_g.reshape(1, d))


def _prep_w_in_kernel(a_ref, b_ref, o_ref):
    j = pl.program_id(0)
    shift = _O_GA % SLAB
    n_gate = 3 * N_Q_HEADS
    n_layers = a_ref.shape[1]

    @pl.when(j < SL_RX)
    def _():
        for l in range(n_layers):
            at = a_ref[:, l, :].T
            bt = b_ref[:, l, :].T
            o_ref[l] = jnp.concatenate([at[:, shift:], bt[:, :shift]], axis=1).astype(o_ref.dtype)

    @pl.when((j >= SL_RX) & (j < SL_CG))
    def _():
        for l in range(n_layers):
            o_ref[l] = a_ref[:, l, :].T.astype(o_ref.dtype)

    @pl.when(j == SL_CG)
    def _():
        lane = lax.broadcasted_iota(jnp.int32, (D_MODEL, SLAB), 1)
        for l in range(n_layers):
            o_ref[l] = jnp.where(lane < n_gate, a_ref[:, l, :].T, 0.0).astype(o_ref.dtype)


def _permute_w_in_all(w_all):
    n_layers, d, n_in = w_all.shape
    w_t = jnp.transpose(w_all, (2, 0, 1))

    def src_block(j):
        blk = (_O_GA // SLAB) + j
        blk = jnp.where(j >= SL_RX, _O_RX // SLAB + (j - SL_RX), blk)
        blk = jnp.where(j >= SL_Q, _O_Q // SLAB + (j - SL_Q), blk)
        blk = jnp.where(j >= SL_AV, _O_AV // SLAB + (j - SL_AV), blk)
        blk = jnp.where(j >= SL_VS, _O_VS // SLAB + (j - SL_VS), blk)
        return jnp.where(j >= SL_CG, _O_CG // SLAB, blk)

    last_blk = (n_in - 1) // SLAB
    return pl.pallas_call(
        _prep_w_in_kernel,
        grid=(N_SLABS,),
        in_specs=[pl.BlockSpec((SLAB, n_layers, d), lambda j: (src_block(j), 0, 0)),
                  pl.BlockSpec((SLAB, n_layers, d),
                               lambda j: (jnp.where(j < SL_RX, src_block(j) + 1, last_blk), 0, 0))],
        out_specs=pl.BlockSpec((n_layers, d, SLAB), lambda j: (0, 0, j)),
        out_shape=jax.ShapeDtypeStruct((n_layers, d, N_IN_PAD), BF16),
        compiler_params=_cparams(("parallel",)),
        name="prep_w_in",
    )(w_t, w_t)


def _rope_tables(s):
    inv = 1.0 / (ROPE_THETA ** (jnp.arange(0, HEAD_DIM, 2, dtype=F32) / HEAD_DIM))
    ang = jnp.arange(s, dtype=F32)[:, None] * inv[None, :]
    cos, sin = jnp.cos(ang), jnp.sin(ang)
    cos_f = jnp.concatenate([cos, cos], axis=-1)
    sin_f = jnp.concatenate([-sin, sin], axis=-1)
    tiles = lambda a: a.reshape(s // ATT_TQ, ATT_TQ, HEAD_DIM).transpose(0, 2, 1)
    return cos_f, sin_f, tiles(cos_f), tiles(sin_f)


def _layer(x2, tables, w_in_bf_all, side_weights, layer, final_norm_g,
           attn_norm_g, conv_dw_w, conv_dw_b,
           conv_ln_g, conv_ln_b, rnn_conv_w, rnn_conv_b, rglru_wa, rglru_ba, rglru_wx, rglru_bx,
           rglru_lambda, cmp_pe, cmp_k_w1, cmp_k_w2, cmp_v_w1, cmp_v_w2, mlp_norm_g):
    is_last = layer == DEPTH - 1
    proj, (w_conv_out, w_rnn_out, w_attn_out, w_o, w_mlp_up, w_mlp_down) = _in_projection(
        x2, attn_norm_g, w_in_bf_all, side_weights, layer)
    ua = _conv_branch(proj, conv_dw_w, conv_dw_b, conv_ln_g, conv_ln_b)
    ub = _rglru_branch(proj, rnn_conv_w, rnn_conv_b, rglru_wa, rglru_ba, rglru_wx, rglru_bx,
                       rglru_lambda)
    kc, vc = _compress(proj, cmp_pe, cmp_k_w1, cmp_k_w2, cmp_v_w1, cmp_v_w2)
    uc = _attention(proj, kc, vc, tables)
    x2 = _merge(ua, ub, uc, proj, x2, w_conv_out, w_rnn_out, w_attn_out, w_o)
    return _mlp(x2, mlp_norm_g, w_mlp_up, w_mlp_down, final_norm_g, is_last)


def kernel(x, attn_norm_g, w_in, conv_dw_w, conv_dw_b, conv_ln_g, conv_ln_b, w_conv_out, rnn_conv_w, rnn_conv_b, rglru_wa, rglru_ba, rglru_wx, rglru_bx, rglru_lambda, w_rnn_out, cmp_pe, cmp_k_w1, cmp_k_w2, cmp_v_w1, cmp_v_w2, w_attn_out, w_o, mlp_norm_g, w_mlp_up, w_mlp_down, final_norm_g):
    b, s, d = x.shape
    assert (b, s, d) == (BATCH, SEQ, D_MODEL)
    tables = _rope_tables(s)
    x2 = x.reshape(b * s, d)
    per_layer = (attn_norm_g, conv_dw_w, conv_dw_b, conv_ln_g, conv_ln_b,
                 rnn_conv_w, rnn_conv_b, rglru_wa, rglru_ba, rglru_wx, rglru_bx, rglru_lambda,
                 cmp_pe, cmp_k_w1, cmp_k_w2, cmp_v_w1, cmp_v_w2, mlp_norm_g)
    side_weights = (w_conv_out, w_rnn_out, w_attn_out, w_o, w_mlp_up, w_mlp_down)
    w_in_bf_all = _permute_w_in_all(w_in)
    for l in range(DEPTH):
        x2 = _layer(x2, tables, w_in_bf_all, side_weights, l, final_norm_g, *[p[l] for p in per_layer])
    return x2.reshape(b, s, d)
```

```python
import functools

import numpy as np
import jax
import jax.numpy as jnp
from jax import lax
from jax.experimental import pallas as pl
from jax.experimental.pallas import tpu as pltpu

F32 = jnp.float32
BF16 = jnp.bfloat16

D_MODEL = 2048
BATCH = 8
SEQ = 2048
DEPTH = 2

D_CONV = D_MODEL // 4
CONV_WIDTH = 31
D_RNN = 3 * D_MODEL // 8
RNN_BLOCKS = 6
RNN_BLOCK_W = D_RNN // RNN_BLOCKS
RNN_CONV_WIDTH = 4
RG_C = 8.0
N_Q_HEADS = 6
N_KV_HEADS = 2
HEAD_DIM = 128
GROUP = N_Q_HEADS // N_KV_HEADS
D_ATTN = N_Q_HEADS * HEAD_DIM
KV_W = N_KV_HEADS * HEAD_DIM
CMP_BLOCK = 32
CMP_STRIDE = 16
SEL_BLOCK = 64
SEL_TOP_N = 16
WINDOW = 512
ROPE_THETA = 10000.0
D_FF = 4 * D_MODEL
NORM_EPS = 1e-6
NEG_INF = -1e30
POS_INF = 1e30

N_CMP_PAD = SEQ // CMP_STRIDE
N_SEL = SEQ // SEL_BLOCK

LANES = 128
BF16_ROWS = 16
SLAB = 256
SL_GA, SL_GB, SL_GC = 0, 8, 16
SL_RX, SL_RG = 24, 27
SL_Q = 30
SL_KC, SL_VC, SL_KS = 33, 34, 35
SL_AV, SL_AG = 36, 38
SL_VS, SL_KW, SL_VW, SL_CG = 40, 41, 42, 43
N_SLABS = 44
N_IN_PAD = N_SLABS * SLAB

_IN_SIZES = (D_CONV, D_CONV, D_RNN, D_RNN, D_ATTN, KV_W, KV_W, KV_W, KV_W, KV_W, KV_W,
             3 * N_Q_HEADS, D_MODEL, D_MODEL, D_MODEL)
_IN_OFF = np.concatenate([[0], np.cumsum(_IN_SIZES)])
(_O_AV, _O_AG, _O_RX, _O_RG, _O_Q, _O_KC, _O_VC, _O_KS, _O_VS, _O_KW, _O_VW, _O_CG,
 _O_GA, _O_GB, _O_GC) = [int(v) for v in _IN_OFF[:-1]]
N_IN = int(_IN_OFF[-1])

VMEM_LIMIT = 62 * 1024 * 1024


def _cparams(sem, vmem=VMEM_LIMIT):
    return pltpu.CompilerParams(dimension_semantics=sem, vmem_limit_bytes=vmem)


def _sigmoid(x):
    return 0.5 * jnp.tanh(0.5 * x) + 0.5


def _gelu_tanh(x):
    c = np.float32(np.sqrt(2.0 / np.pi))
    return 0.5 * x * (1.0 + jnp.tanh(c * (x + 0.044715 * (x * x * x))))


IN_TM = 1024
IN_TN = N_IN_PAD // 4
NORM_RC = 128


def _rmsnorm_to(h_ref, x_ref, g_ref, rows):
    g = g_ref[...]

    def body(c, carry):
        r0 = pl.multiple_of(c * NORM_RC, NORM_RC)
        x = x_ref[pl.ds(r0, NORM_RC), :]
        ms = jnp.mean(x * x, axis=-1, keepdims=True)
        h_ref[pl.ds(r0, NORM_RC), :] = (x * lax.rsqrt(ms + NORM_EPS) * g).astype(h_ref.dtype)
        return carry

    lax.fori_loop(0, rows // NORM_RC, body, 0)


N_SIDE = 6


def _inproj_kernel(x_ref, g_ref, w_ref, *refs):
    side_in, o_ref, side_out, h_ref = refs[:N_SIDE], refs[N_SIDE], refs[N_SIDE + 1:-1], refs[-1]

    @pl.when(pl.program_id(1) == 0)
    def _():
        _rmsnorm_to(h_ref, x_ref, g_ref, IN_TM)

    for src, dst in zip(side_in, side_out):
        dst[...] = src[...].astype(dst.dtype)

    for k in range(IN_TN // SLAB):
        r = jnp.dot(h_ref[...], w_ref[:, k * SLAB:(k + 1) * SLAB], preferred_element_type=F32)
        o_ref[k] = r.astype(o_ref.dtype)


def _in_projection(x2, g, w_perm_all, side_weights, layer):
    rows, d = x2.shape
    n_i, n_j = rows // IN_TM, N_IN_PAD // IN_TN
    assert len(side_weights) == N_SIDE
    side_in, side_out, side_shapes = [], [], []
    for w in side_weights:
        _, r, c = w.shape
        pr = max(r // (n_i * n_j), BF16_ROWS)
        assert r % pr == 0 and pr % BF16_ROWS == 0
        piece = lambda i, j, last=r // pr - 1: jnp.minimum(i * n_j + j, last)
        side_in.append(pl.BlockSpec((None, pr, c), lambda i, j, piece=piece: (layer, piece(i, j), 0)))
        side_out.append(pl.BlockSpec((pr, c), lambda i, j, piece=piece: (piece(i, j), 0)))
        side_shapes.append(jax.ShapeDtypeStruct((r, c), BF16))
    out = pl.pallas_call(
        _inproj_kernel,
        grid=(n_i, n_j),
        in_specs=[pl.BlockSpec((IN_TM, d), lambda i, j: (i, 0)),
                  pl.BlockSpec((1, d), lambda i, j: (0, 0)),
                  pl.BlockSpec((None, d, IN_TN), lambda i, j: (layer, 0, j))] + side_in,
        out_specs=[pl.BlockSpec((IN_TN // SLAB, IN_TM, SLAB), lambda i, j: (j, i, 0))] + side_out,
        out_shape=[jax.ShapeDtypeStruct((N_SLABS, rows, SLAB), BF16)] + side_shapes,
        scratch_shapes=[pltpu.VMEM((IN_TM, d), BF16)],
        compiler_params=_cparams(("parallel", "arbitrary")),
        name="in_projection",
    )(x2, g.reshape(1, d), w_perm_all, *side_weights)
    return out[0], out[1:]


MIX_TS = 256
MIX_TR = MIX_TS * BATCH
CONV_HALO = 256
CONV_RC = 64
CONV_PARTS = 4


def _conv_kernel(v_ref, g_ref, w_ref, b_ref, lg_ref, lb_ref, o_ref, ubuf, ybuf):
    nc = D_CONV // LANES
    per_slab = SLAB // LANES

    @pl.when(pl.program_id(0) == 0)
    def _():
        ubuf[:, 0:CONV_HALO, :] = jnp.zeros((nc, CONV_HALO, LANES), F32)

    def glu(b, carry):
        for c in range(nc):
            k, ls = c // per_slab, slice((c % per_slab) * LANES, (c % per_slab + 1) * LANES)
            v = v_ref[k, b, :, ls].astype(F32)
            g = g_ref[k, b, :, ls].astype(F32)
            ubuf[c, pl.ds(CONV_HALO + b, MIX_TS, stride=BATCH), :] = v * _sigmoid(g)
        return carry

    lax.fori_loop(0, BATCH, glu, 0)

    base = CONV_HALO - (CONV_WIDTH - 1) * BATCH

    def conv(i, carry):
        r0 = pl.multiple_of(i * CONV_RC, CONV_RC)
        rows = pl.ds(r0, CONV_RC)
        total = jnp.zeros((CONV_RC, 1), F32)
        for c in range(nc):
            ls = slice(c * LANES, (c + 1) * LANES)
            parts = [None] * CONV_PARTS
            for j in range(CONV_WIDTH):
                term = w_ref[j:j + 1, ls] * ubuf[c, pl.ds(r0 + base + BATCH * j, CONV_RC), :]
                k = j % CONV_PARTS
                parts[k] = term if parts[k] is None else parts[k] + term
            a = (parts[0] + parts[1]) + (parts[2] + parts[3]) + b_ref[:, ls]
            ybuf[c, rows, :] = a
            total = total + jnp.sum(a, axis=-1, keepdims=True)
        mu = total * (1.0 / D_CONV)
        sq = jnp.zeros((CONV_RC, 1), F32)
        for c in range(nc):
            cen = ybuf[c, rows, :] - mu
            sq = sq + jnp.sum(cen * cen, axis=-1, keepdims=True)
        inv = lax.rsqrt(sq * (1.0 / D_CONV) + NORM_EPS)
        for c in range(nc):
            ls = slice(c * LANES, (c + 1) * LANES)
            y = (ybuf[c, rows, :] - mu) * inv * lg_ref[:, ls] + lb_ref[:, ls]
            ybuf[c, rows, :] = y * _sigmoid(y)
        return carry

    lax.fori_loop(0, MIX_TR // CONV_RC, conv, 0)

    def put(b, carry):
        for c in range(nc):
            k, ls = c // per_slab, slice((c % per_slab) * LANES, (c % per_slab + 1) * LANES)
            o_ref[k, b, :, ls] = ybuf[c, pl.ds(b, MIX_TS, stride=BATCH), :].astype(o_ref.dtype)
        return carry

    lax.fori_loop(0, BATCH, put, 0)
    ubuf[:, 0:CONV_HALO, :] = ubuf[:, MIX_TR:MIX_TR + CONV_HALO, :]


def _conv_branch(proj, w, b, lg, lb):
    seq = proj.shape[1] // BATCH
    p4 = proj.reshape(N_SLABS, BATCH, seq, SLAB)
    nk = D_CONV // SLAB
    vec = lambda: pl.BlockSpec((1, D_CONV), lambda i: (0, 0))
    out = pl.pallas_call(
        _conv_kernel,
        grid=(seq // MIX_TS,),
        in_specs=[pl.BlockSpec((nk, BATCH, MIX_TS, SLAB), lambda i: (SL_AV // nk, 0, i, 0)),
                  pl.BlockSpec((nk, BATCH, MIX_TS, SLAB), lambda i: (SL_AG // nk, 0, i, 0)),
                  pl.BlockSpec((CONV_WIDTH, D_CONV), lambda i: (0, 0)),
                  vec(), vec(), vec()],
        out_specs=pl.BlockSpec((nk, BATCH, MIX_TS, SLAB), lambda i: (0, 0, i, 0)),
        out_shape=jax.ShapeDtypeStruct((nk, BATCH, seq, SLAB), BF16),
        scratch_shapes=[pltpu.VMEM((D_CONV // LANES, CONV_HALO + MIX_TR, LANES), F32),
                        pltpu.VMEM((D_CONV // LANES, MIX_TR, LANES), F32)],
        compiler_params=_cparams(("arbitrary",)),
        name="conv_branch",
    )(p4, p4, w, b.reshape(1, -1), lg.reshape(1, -1), lb.reshape(1, -1))
    return out.reshape(nk, BATCH * seq, SLAB)


RNN_HALO = 32
RNN_RC = 256


def _rglru_kernel(x_ref, gate_ref, cw_ref, cb_ref, wa_ref, ba_ref, wx_ref, bx_ref, lam_ref,
                  o_ref, xbuf, abuf, gbuf, hstate):
    per_slab = SLAB // RNN_BLOCK_W

    @pl.when(pl.program_id(0) == 0)
    def _():
        xbuf[:, 0:RNN_HALO, :] = jnp.zeros((RNN_BLOCKS, RNN_HALO, RNN_BLOCK_W), F32)
        hstate[...] = jnp.zeros_like(hstate)

    def slab_cols(n):
        return n // per_slab, slice((n % per_slab) * RNN_BLOCK_W, (n % per_slab + 1) * RNN_BLOCK_W)

    def load(b, carry):
        for n in range(RNN_BLOCKS):
            k, ls = slab_cols(n)
            xbuf[n, pl.ds(RNN_HALO + b, MIX_TS, stride=BATCH), :] = x_ref[k, b, :, ls].astype(F32)
        return carry

    lax.fori_loop(0, BATCH, load, 0)

    z = -lam_ref[...]
    softplus = jnp.maximum(z, 0.0) + jnp.log(1.0 + jnp.exp(-jnp.abs(z)))
    coef = -RG_C * softplus
    base = RNN_HALO - (RNN_CONV_WIDTH - 1) * BATCH

    def gates(i, carry):
        r0 = pl.multiple_of(i * RNN_RC, RNN_RC)
        for n in range(RNN_BLOCKS):
            cs = slice(n * RNN_BLOCK_W, (n + 1) * RNN_BLOCK_W)
            y = jnp.zeros((RNN_RC, RNN_BLOCK_W), F32)
            for j in range(RNN_CONV_WIDTH):
                y = y + cw_ref[j:j + 1, cs] * xbuf[n, pl.ds(r0 + base + BATCH * j, RNN_RC), :]
            y = y + cb_ref[:, cs]
            yb = y.astype(BF16)
            ra = _sigmoid(jnp.dot(yb, wa_ref[n], preferred_element_type=F32) + ba_ref[:, cs])
            ri = _sigmoid(jnp.dot(yb, wx_ref[n], preferred_element_type=F32) + bx_ref[:, cs])
            a = jnp.exp(coef[:, cs] * ra)
            abuf[n, pl.ds(r0, RNN_RC), :] = a
            gbuf[n, pl.ds(r0, RNN_RC), :] = jnp.sqrt(1.0 - a * a) * (ri * y)
        return carry

    lax.fori_loop(0, MIX_TR // RNN_RC, gates, 0)

    def step(t, h):
        r0 = pl.multiple_of(t * BATCH, BATCH)
        h = abuf[:, pl.ds(r0, BATCH), :] * h + gbuf[:, pl.ds(r0, BATCH), :]
        gbuf[:, pl.ds(r0, BATCH), :] = h
        return h

    hstate[...] = lax.fori_loop(0, MIX_TS, step, hstate[...], unroll=8)

    def put(b, carry):
        for n in range(RNN_BLOCKS):
            k, ls = slab_cols(n)
            h = gbuf[n, pl.ds(b, MIX_TS, stride=BATCH), :]
            o_ref[k, b, :, ls] = (h * _gelu_tanh(gate_ref[k, b, :, ls].astype(F32))).astype(o_ref.dtype)
        return carry

    lax.fori_loop(0, BATCH, put, 0)
    xbuf[:, 0:RNN_HALO, :] = xbuf[:, MIX_TR:MIX_TR + RNN_HALO, :]


def _rglru_branch(proj, cw, cb, wa, ba, wx, bx, lam):
    seq = proj.shape[1] // BATCH
    p4 = proj.reshape(N_SLABS, BATCH, seq, SLAB)
    nk = D_RNN // SLAB
    vec = lambda: pl.BlockSpec((1, D_RNN), lambda i: (0, 0))
    blk = lambda: pl.BlockSpec((RNN_BLOCKS, RNN_BLOCK_W, RNN_BLOCK_W), lambda i: (0, 0, 0))
    out = pl.pallas_call(
        _rglru_kernel,
        grid=(seq // MIX_TS,),
        in_specs=[pl.BlockSpec((nk, BATCH, MIX_TS, SLAB), lambda i: (SL_RX // nk, 0, i, 0)),
                  pl.BlockSpec((nk, BATCH, MIX_TS, SLAB), lambda i: (SL_RG // nk, 0, i, 0)),
                  pl.BlockSpec((RNN_CONV_WIDTH, D_RNN), lambda i: (0, 0)),
                  vec(), blk(), vec(), blk(), vec(), vec()],
        out_specs=pl.BlockSpec((nk, BATCH, MIX_TS, SLAB), lambda i: (0, 0, i, 0)),
        out_shape=jax.ShapeDtypeStruct((nk, BATCH, seq, SLAB), BF16),
        scratch_shapes=[pltpu.VMEM((RNN_BLOCKS, RNN_HALO + MIX_TR, RNN_BLOCK_W), F32),
                        pltpu.VMEM((RNN_BLOCKS, MIX_TR, RNN_BLOCK_W), F32),
                        pltpu.VMEM((RNN_BLOCKS, MIX_TR, RNN_BLOCK_W), F32),
                        pltpu.VMEM((RNN_BLOCKS, BATCH, RNN_BLOCK_W), F32)],
        compiler_params=_cparams(("arbitrary",)),
        name="rglru_branch",
    )(p4, p4, cw, cb.reshape(1, -1), wa.astype(BF16), ba.reshape(1, -1),
      wx.astype(BF16), bx.reshape(1, -1), lam.reshape(1, -1))
    return out.reshape(nk, BATCH * seq, SLAB)


def _compress_kernel(k_ref, v_ref, pe_ref, kw1_ref, vw1_ref, kw2_ref, vw2_ref, kc_ref, vc_ref, stage):
    half = CMP_STRIDE * HEAD_DIM
    pe = pe_ref[...].astype(BF16)
    for src, w1_ref, w2_ref, dst in ((k_ref, kw1_ref, kw2_ref, kc_ref),
                                     (v_ref, vw1_ref, vw2_ref, vc_ref)):
        w1 = w1_ref[...]
        pe_term = (jnp.dot(pe[:, :half], w1[:, :HEAD_DIM], preferred_element_type=F32)
                   + jnp.dot(pe[:, half:], w1[:, HEAD_DIM:], preferred_element_type=F32))
        for h in range(N_KV_HEADS):
            stage[...] = src[:, h * HEAD_DIM:(h + 1) * HEAD_DIM].astype(F32)
            x = jnp.concatenate([stage[pl.ds(l, N_CMP_PAD, stride=CMP_STRIDE), :].astype(BF16)
                                 for l in range(CMP_STRIDE)], axis=1)
            p = jnp.dot(x, w1, preferred_element_type=F32)
            hi_next = pltpu.roll(p[:, HEAD_DIM:], N_CMP_PAD - 1, 0)
            pre = p[:, :HEAD_DIM] + hi_next + pe_term[0:1, :]
            y = jnp.dot(_gelu_tanh(pre).astype(BF16), w2_ref[...], preferred_element_type=F32)
            dst[:, h * HEAD_DIM:(h + 1) * HEAD_DIM] = y.astype(dst.dtype)


def _compress(proj, pe, kw1, kw2, vw1, vw2):
    seq = proj.shape[1] // BATCH
    half = CMP_STRIDE * HEAD_DIM
    pe2 = jnp.zeros((8, CMP_BLOCK * HEAD_DIM), F32).at[0].set(pe.reshape(-1))

    def w1cat(w1):
        return jnp.concatenate([w1[:half], w1[half:]], axis=-1).astype(BF16)

    kv_spec = lambda sl: pl.BlockSpec((None, seq, SLAB), lambda b: (sl, b, 0))
    w1_spec = lambda: pl.BlockSpec((half, 2 * HEAD_DIM), lambda b: (0, 0))
    w2_spec = lambda: pl.BlockSpec((HEAD_DIM, HEAD_DIM), lambda b: (0, 0))
    out_spec = lambda: pl.BlockSpec((None, N_CMP_PAD, KV_W), lambda b: (b, 0, 0))
    return pl.pallas_call(
        _compress_kernel,
        grid=(BATCH,),
        in_specs=[kv_spec(SL_KC), kv_spec(SL_VC),
                  pl.BlockSpec((8, CMP_BLOCK * HEAD_DIM), lambda b: (0, 0)),
                  w1_spec(), w1_spec(), w2_spec(), w2_spec()],
        out_specs=[out_spec(), out_spec()],
        out_shape=[jax.ShapeDtypeStruct((BATCH, N_CMP_PAD, KV_W), BF16)] * 2,
        scratch_shapes=[pltpu.VMEM((seq, HEAD_DIM), F32)],
        compiler_params=_cparams(("parallel",)),
        name="compress_kv",
    )(proj, proj, pe2, w1cat(kw1), w1cat(vw1), kw2.astype(BF16), vw2.astype(BF16))


ATT_TQ = 256
ATT_TK = 256
ATT_NKB = SEQ // ATT_TK
ATT_WB = WINDOW // ATT_TK + 1
ATT_TS = 2 * ATT_TK
ATT_LANES = GROUP * ATT_TQ


def _tile_heads(x):
    return jnp.concatenate([x] * GROUP, axis=1)


def _attn_kernel(q_ref, ks_ref, vs_ref, kw_ref, vw_ref, kc_ref, vc_ref, cg_ref, cos_ref, sin_ref,
                 cost_ref, sint_ref, ovt_ref, o_ref, ksr, kwr, vst, vwt, vct, qaug, score_scr):
    qi = pl.program_id(1)
    scale = np.float32(HEAD_DIM ** -0.5 * np.log2(np.e))
    half = HEAD_DIM // 2

    @pl.when(qi == 0)
    def _():
        def prep(c, carry):
            r0 = pl.multiple_of(c * ATT_TK, ATT_TK)
            cos = cos_ref[pl.ds(r0, ATT_TK), :]
            sin = sin_ref[pl.ds(r0, ATT_TK), :]
            blk = (r0 + lax.broadcasted_iota(jnp.int32, (ATT_TK, HEAD_DIM), 0)) >> 6
            onehot = jnp.where(lax.broadcasted_iota(jnp.int32, (ATT_TK, HEAD_DIM), 1) == blk, 1.0, 0.0)
            for h in range(N_KV_HEADS):
                cs = slice(h * HEAD_DIM, (h + 1) * HEAD_DIM)
                x = ks_ref[pl.ds(r0, ATT_TK), cs].astype(F32)
                ksr[h, pl.ds(r0, ATT_TK), 0:HEAD_DIM] = (
                    x * cos + pltpu.roll(x, half, 1) * sin).astype(ksr.dtype)
                ksr[h, pl.ds(r0, ATT_TK), HEAD_DIM:2 * HEAD_DIM] = onehot.astype(ksr.dtype)
                x = kw_ref[pl.ds(r0, ATT_TK), cs].astype(F32)
                kwr[pl.ds(r0, ATT_TK), cs] = (x * cos + pltpu.roll(x, half, 1) * sin).astype(kwr.dtype)
                vwt[h, c] = vw_ref[pl.ds(r0, ATT_TK), cs].astype(F32).T.astype(vwt.dtype)
            return carry

        lax.fori_loop(0, ATT_NKB, prep, 0)

        def prep_vs(c, carry):
            for h in range(N_KV_HEADS):
                cs = slice(h * HEAD_DIM, (h + 1) * HEAD_DIM)
                parts = []
                for i in range(ATT_TS // ATT_TK):
                    r0 = pl.multiple_of(c * ATT_TS + i * ATT_TK, ATT_TK)
                    parts.append(vs_ref[pl.ds(r0, ATT_TK), cs].astype(F32).T.astype(vst.dtype))
                vst[h, c] = jnp.concatenate(parts, axis=1)
            return carry

        lax.fori_loop(0, SEQ // ATT_TS, prep_vs, 0)
        for h in range(N_KV_HEADS):
            vct[h] = vc_ref[:, h * HEAD_DIM:(h + 1) * HEAD_DIM].astype(F32).T.astype(vct.dtype)
        qaug[:, HEAD_DIM + N_SEL:, :] = jnp.zeros(
            (N_KV_HEADS, HEAD_DIM - N_SEL, ATT_LANES), qaug.dtype)

    t0 = qi * ATT_TQ
    cos_t = cost_ref[...]
    sin_t = sint_ref[...]
    gate_t = _sigmoid(cg_ref[:, 0:HEAD_DIM].astype(F32).T)

    t_c = t0 + (lax.broadcasted_iota(jnp.int32, (N_CMP_PAD, ATT_LANES), 1) & (ATT_TQ - 1))
    n_sub = lax.broadcasted_iota(jnp.int32, (N_CMP_PAD, ATT_LANES), 0)
    valid_c = (n_sub * CMP_STRIDE + (CMP_BLOCK - 1)) <= t_c
    t_s = t0 + lax.broadcasted_iota(jnp.int32, (N_SEL, ATT_TQ), 1)
    m_sub = lax.broadcasted_iota(jnp.int32, (N_SEL, ATT_TQ), 0)
    cur = t_s >> 6
    valid_s = m_sub <= cur
    forced = (m_sub == 0) | (m_sub == cur) | (m_sub == cur - 1)
    w_blk = jnp.maximum(qi - WINDOW // ATT_TK, 0)
    w_start = pl.multiple_of(w_blk * ATT_TK, ATT_TK)
    dist = ((t0 + lax.broadcasted_iota(jnp.int32, (ATT_WB * ATT_TK, ATT_TQ), 1))
            - (w_start + lax.broadcasted_iota(jnp.int32, (ATT_WB * ATT_TK, ATT_TQ), 0)))
    win_bias = jnp.where((dist >= 0) & (dist < WINDOW), 0.0, NEG_INF)

    heads = range(N_KV_HEADS)
    head_cols = [slice(h * HEAD_DIM, (h + 1) * HEAD_DIM) for h in heads]
    qn3 = []
    for h in heads:
        q_plain, q_rot = [], []
        for g in range(GROUP):
            hq = h * GROUP + g
            xt = q_ref[hq // 2, :, (hq % 2) * HEAD_DIM:(hq % 2 + 1) * HEAD_DIM].astype(F32).T
            swapped = jnp.concatenate([xt[half:], xt[:half]], axis=0)
            q_plain.append((xt * scale).astype(BF16))
            q_rot.append(((xt * cos_t + swapped * sin_t) * scale).astype(BF16))
        qn3.append(jnp.concatenate(q_plain, axis=1))
        qaug[h, 0:HEAD_DIM, :] = jnp.concatenate(q_rot, axis=1)

    s_cmp = [jnp.where(valid_c, jnp.dot(kc_ref[:, head_cols[h]], qn3[h], preferred_element_type=F32),
                       NEG_INF) for h in heads]
    win_bias3 = _tile_heads(win_bias)
    s_win = [jnp.dot(kwr[pl.ds(w_start, ATT_WB * ATT_TK), head_cols[h]], qaug[h, 0:HEAD_DIM, :],
                     preferred_element_type=F32) + win_bias3 for h in heads]
    p_cmp = []
    for h in heads:
        mx = jnp.max(s_cmp[h], axis=0, keepdims=True)
        e = jnp.where(valid_c, jnp.exp2(s_cmp[h] - mx), 0.0)
        den = jnp.sum(e, axis=0, keepdims=True)
        p_cmp.append(e / jnp.where(den > 0.0, den, 1.0))
    p_win, den_win = [], []
    for h in heads:
        pw = jnp.exp2(s_win[h] - jnp.max(s_win[h], axis=0, keepdims=True))
        den_win.append(jnp.sum(pw, axis=0, keepdims=True))
        p_win.append(pw.astype(BF16))
    o_cmp = [jnp.dot(vct[h], p_cmp[h].astype(BF16), preferred_element_type=F32) for h in heads]
    for h in heads:
        p_sum = p_cmp[h][:, 0:ATT_TQ]
        for g in range(1, GROUP):
            p_sum = p_sum + p_cmp[h][:, g * ATT_TQ:(g + 1) * ATT_TQ]
        imp = jnp.dot(ovt_ref[...], p_sum, preferred_element_type=F32,
                      precision=lax.Precision.HIGHEST)
        score_scr[h] = jnp.where(valid_s, jnp.where(forced, POS_INF, imp), NEG_INF)
    o_win = []
    for h in heads:
        acc_w = None
        for i in range(ATT_WB):
            part = jnp.dot(vwt[h, w_blk + i], p_win[h][i * ATT_TK:(i + 1) * ATT_TK],
                           preferred_element_type=F32)
            acc_w = part if acc_w is None else acc_w + part
        o_win.append(acc_w / den_win[h])
    m_v = lax.broadcasted_iota(jnp.int32, (8, ATT_TQ), 0)
    sel_rows = [[] for _ in heads]
    for v in range(N_SEL // 8):
        sc_v = [score_scr[h, 8 * v:8 * v + 8, :] for h in heads]
        cnt = [jnp.zeros((8, ATT_TQ), F32) for _ in heads]
        for mp in range(N_SEL):
            for h in heads:
                row = score_scr[h, mp:mp + 1, :]
                if mp < 8 * v:
                    beats = row >= sc_v[h]
                elif mp >= 8 * v + 8:
                    beats = row > sc_v[h]
                else:
                    beats = (row > sc_v[h]) | ((row == sc_v[h]) & (m_v > mp - 8 * v))
                cnt[h] = cnt[h] + jnp.where(beats, 1.0, 0.0)
        for h in heads:
            keep = (cnt[h] < SEL_TOP_N) & (sc_v[h] > NEG_INF)
            sel_rows[h].append(jnp.where(keep, 0.0, NEG_INF))
    for h in heads:
        sel_bias = jnp.concatenate(sel_rows[h], axis=0)
        qaug[h, HEAD_DIM:HEAD_DIM + N_SEL, :] = _tile_heads(sel_bias).astype(qaug.dtype)

    def sel_step(j, carry, causal):
        k0 = pl.multiple_of(j * ATT_TS, ATT_TS)
        if causal:
            kpos = k0 + lax.broadcasted_iota(jnp.int32, (ATT_TS, ATT_TQ), 0)
            tpos = t0 + lax.broadcasted_iota(jnp.int32, (ATT_TS, ATT_TQ), 1)
            causal_bias = _tile_heads(jnp.where(kpos <= tpos, 0.0, NEG_INF))
        out = []
        for h in range(N_KV_HEADS):
            m_run, l_run, acc = carry[h]
            sc = jnp.dot(ksr[h, pl.ds(k0, ATT_TS), :], qaug[h], preferred_element_type=F32)
            if causal:
                sc = sc + causal_bias
            m_new = jnp.maximum(m_run, jnp.max(sc, axis=0, keepdims=True))
            alpha = jnp.exp2(m_run - m_new)
            pr = jnp.exp2(sc - m_new)
            l_new = alpha * l_run + jnp.sum(pr, axis=0, keepdims=True)
            acc = alpha * acc + jnp.dot(vst[h, j], pr.astype(BF16), preferred_element_type=F32)
            out.append((m_new, l_new, acc))
        return tuple(out)

    init = (jnp.full((1, ATT_LANES), NEG_INF, F32), jnp.zeros((1, ATT_LANES), F32),
            jnp.zeros((HEAD_DIM, ATT_LANES), F32))
    last = (t0 + ATT_TQ - 1) // ATT_TS
    carry = sel_step(last, (init,) * N_KV_HEADS, True)
    sel_out = lax.fori_loop(0, last, lambda j, c: sel_step(j, c, False), carry)

    for h in heads:
        _, l_s, acc_s = sel_out[h]
        o_sel = acc_s / l_s
        for g in range(GROUP):
            hq = h * GROUP + g
            ls = slice(g * ATT_TQ, (g + 1) * ATT_TQ)
            out_t = (gate_t[3 * hq:3 * hq + 1, :] * o_cmp[h][:, ls]
                     + gate_t[3 * hq + 1:3 * hq + 2, :] * o_sel[:, ls]
                     + gate_t[3 * hq + 2:3 * hq + 3, :] * o_win[h][:, ls])
            o_ref[hq // 2, :, (hq % 2) * HEAD_DIM:(hq % 2 + 1) * HEAD_DIM] = out_t.T.astype(o_ref.dtype)


def _attention(proj, kc, vc, tables):
    cos, sin, cos_t, sin_t = tables
    rows = proj.shape[1]
    seq = rows // BATCH
    nq = seq // ATT_TQ
    c_start = np.arange(N_CMP_PAD) * CMP_STRIDE
    s_start = np.arange(N_SEL) * SEL_BLOCK
    ovt = ((c_start[None, :] < s_start[:, None] + SEL_BLOCK)
           & (c_start[None, :] + CMP_BLOCK > s_start[:, None])
           & (np.arange(N_CMP_PAD)[None, :] < N_CMP_PAD - 1)).astype(np.float32)

    full = lambda sl: pl.BlockSpec((None, seq, SLAB), lambda b, qi: (sl, b, 0))
    cmp_spec = lambda: pl.BlockSpec((None, N_CMP_PAD, KV_W), lambda b, qi: (b, 0, 0))
    tab = lambda: pl.BlockSpec((seq, HEAD_DIM), lambda b, qi: (0, 0))
    tab_t = lambda: pl.BlockSpec((None, HEAD_DIM, ATT_TQ), lambda b, qi: (qi, 0, 0))
    nq3 = D_ATTN // SLAB
    kv_t = lambda tk: pltpu.VMEM((N_KV_HEADS, seq // tk, HEAD_DIM, tk), BF16)
    return pl.pallas_call(
        _attn_kernel,
        grid=(BATCH, nq),
        in_specs=[pl.BlockSpec((nq3, ATT_TQ, SLAB), lambda b, qi: (SL_Q // nq3, b * nq + qi, 0)),
                  full(SL_KS), full(SL_VS), full(SL_KW), full(SL_VW),
                  cmp_spec(), cmp_spec(),
                  pl.BlockSpec((None, ATT_TQ, SLAB), lambda b, qi: (SL_CG, b * nq + qi, 0)),
                  tab(), tab(), tab_t(), tab_t(),
                  pl.BlockSpec((N_SEL, N_CMP_PAD), lambda b, qi: (0, 0))],
        out_specs=pl.BlockSpec((nq3, ATT_TQ, SLAB), lambda b, qi: (0, b * nq + qi, 0)),
        out_shape=jax.ShapeDtypeStruct((nq3, rows, SLAB), BF16),
        scratch_shapes=[pltpu.VMEM((N_KV_HEADS, seq, 2 * HEAD_DIM), BF16),
                        pltpu.VMEM((seq, KV_W), BF16),
                        kv_t(ATT_TS), kv_t(ATT_TK),
                        pltpu.VMEM((N_KV_HEADS, HEAD_DIM, N_CMP_PAD), BF16),
                        pltpu.VMEM((N_KV_HEADS, 2 * HEAD_DIM, ATT_LANES), BF16),
                        pltpu.VMEM((N_KV_HEADS, N_SEL, ATT_TQ), F32)],
        compiler_params=_cparams(("parallel", "arbitrary")),
        name="sparse_attention",
    )(proj, proj, proj, proj, proj, kc, vc, proj, cos, sin, cos_t, sin_t, jnp.asarray(ovt))


MERGE_TM = 512
MERGE_TN = 512


def _cat_slabs(ref, first, n):
    return jnp.concatenate([ref[first + k] for k in range(n)], axis=-1)


def _merge_kernel(ua_ref, ub_ref, uc_ref, ga_ref, gb_ref, gc_ref, x_ref,
                  wa_ref, wb_ref, wc_ref, wo_ref, o_ref):
    per = MERGE_TN // SLAB
    branches = ((ua_ref, ga_ref, wa_ref), (ub_ref, gb_ref, wb_ref), (uc_ref, gc_ref, wc_ref))
    acts = [_cat_slabs(u_ref, 0, u_ref.shape[0]) for u_ref, _, _ in branches]
    for c in range(D_MODEL // MERGE_TN):
        cs = slice(c * MERGE_TN, (c + 1) * MERGE_TN)
        y = None
        for u, (_, g_ref, w_ref) in zip(acts, branches):
            p = jnp.dot(u, w_ref[:, cs], preferred_element_type=F32)
            term = _sigmoid(_cat_slabs(g_ref, c * per, per).astype(F32)) * p
            y = term if y is None else y + term
        part = jnp.dot(y.astype(BF16), wo_ref[cs, :], preferred_element_type=F32)
        if c == 0:
            o_ref[...] = x_ref[...] + part
        else:
            o_ref[...] += part


def _merge(ua, ub, uc, proj, x2, wa, wb, wc, wo):
    rows, d = x2.shape
    ng = d // SLAB
    act = lambda n: pl.BlockSpec((n, MERGE_TM, SLAB), lambda i: (0, i, 0))
    gate = lambda sl: pl.BlockSpec((ng, MERGE_TM, SLAB), lambda i: (sl // ng, i, 0))
    res = lambda k: pl.BlockSpec((k, d), lambda i: (0, 0), pipeline_mode=pl.Buffered(1))
    return pl.pallas_call(
        _merge_kernel,
        grid=(rows // MERGE_TM,),
        in_specs=[act(ua.shape[0]), act(ub.shape[0]), act(uc.shape[0]),
                  gate(SL_GA), gate(SL_GB), gate(SL_GC),
                  pl.BlockSpec((MERGE_TM, d), lambda i: (i, 0)),
                  res(wa.shape[0]), res(wb.shape[0]), res(wc.shape[0]), res(wo.shape[0])],
        out_specs=pl.BlockSpec((MERGE_TM, d), lambda i: (i, 0)),
        out_shape=jax.ShapeDtypeStruct((rows, d), F32),
        compiler_params=_cparams(("parallel",)),
        name="merge_out_proj",
    )(ua, ub, uc, proj, proj, proj, x2, wa, wb, wc, wo)


MLP_TM = 1024
MLP_TF = 1024


def _mlp_kernel(x_ref, g_ref, wu_ref, wd_ref, og_ref, o_ref, h_ref, *, norm_output):
    @pl.when(pl.program_id(1) == 0)
    def _():
        _rmsnorm_to(h_ref, x_ref, g_ref, MLP_TM)
        o_ref[...] = x_ref[...]

    a = jnp.dot(h_ref[...], wu_ref[...], preferred_element_type=F32)
    a = jnp.maximum(a, 0.0)
    o_ref[...] += jnp.dot((a * a).astype(BF16), wd_ref[...], preferred_element_type=F32)

    if norm_output:
        @pl.when(pl.program_id(1) == pl.num_programs(1) - 1)
        def _():
            _rmsnorm_to(o_ref, o_ref, og_ref, MLP_TM)


def _mlp(x2, g, wu, wd, out_g, norm_output):
    rows, d = x2.shape
    f = wu.shape[1]
    return pl.pallas_call(
        functools.partial(_mlp_kernel, norm_output=norm_output),
        grid=(rows // MLP_TM, f // MLP_TF),
        in_specs=[pl.BlockSpec((MLP_TM, d), lambda i, j: (i, 0)),
                  pl.BlockSpec((1, d), lambda i, j: (0, 0)),
                  pl.BlockSpec((d, MLP_TF), lambda i, j: (0, j)),
                  pl.BlockSpec((MLP_TF, d), lambda i, j: (j, 0)),
                  pl.BlockSpec((1, d), lambda i, j: (0, 0))],
        out_specs=pl.BlockSpec((MLP_TM, d), lambda i, j: (i, 0)),
        out_shape=jax.ShapeDtypeStruct((rows, d), F32),
        scratch_shapes=[pltpu.VMEM((MLP_TM, d), BF16)],
        compiler_params=_cparams(("parallel", "arbitrary")),
        name="mlp",
    )(x2, g.reshape(1, d), wu, wd, out_g.reshape(1, d))


def _prep_w_in_kernel(a_ref, b_ref, o_ref):
    j = pl.program_id(0)
    shift = _O_GA % SLAB
    n_gate = 3 * N_Q_HEADS
    n_layers = a_ref.shape[1]

    @pl.when(j < SL_RX)
    def _():
        for l in range(n_layers):
            at = a_ref[:, l, :].T
            bt = b_ref[:, l, :].T
            o_ref[l] = jnp.concatenate([at[:, shift:], bt[:, :shift]], axis=1).astype(o_ref.dtype)

    @pl.when((j >= SL_RX) & (j < SL_CG))
    def _():
        for l in range(n_layers):
            o_ref[l] = a_ref[:, l, :].T.astype(o_ref.dtype)

    @pl.when(j == SL_CG)
    def _():
        lane = lax.broadcasted_iota(jnp.int32, (D_MODEL, SLAB), 1)
        for l in range(n_layers):
            o_ref[l] = jnp.where(lane < n_gate, a_ref[:, l, :].T, 0.0).astype(o_ref.dtype)


def _permute_w_in_all(w_all):
    n_layers, d, n_in = w_all.shape
    w_t = jnp.transpose(w_all, (2, 0, 1))

    def src_block(j):
        blk = (_O_GA // SLAB) + j
        blk = jnp.where(j >= SL_RX, _O_RX // SLAB + (j - SL_RX), blk)
        blk = jnp.where(j >= SL_Q, _O_Q // SLAB + (j - SL_Q), blk)
        blk = jnp.where(j >= SL_AV, _O_AV // SLAB + (j - SL_AV), blk)
        blk = jnp.where(j >= SL_VS, _O_VS // SLAB + (j - SL_VS), blk)
        return jnp.where(j >= SL_CG, _O_CG // SLAB, blk)

    tail_rows = SLAB // 2
    assert _O_GA % SLAB <= tail_rows
    last_tail = (n_in - 1) // tail_rows
    return pl.pallas_call(
        _prep_w_in_kernel,
        grid=(N_SLABS,),
        in_specs=[pl.BlockSpec((SLAB, n_layers, d), lambda j: (src_block(j), 0, 0)),
                  pl.BlockSpec((tail_rows, n_layers, d),
                               lambda j: (jnp.where(j < SL_RX, 2 * (src_block(j) + 1), last_tail), 0, 0))],
        out_specs=pl.BlockSpec((n_layers, d, SLAB), lambda j: (0, 0, j)),
        out_shape=jax.ShapeDtypeStruct((n_layers, d, N_IN_PAD), BF16),
        compiler_params=_cparams(("parallel",)),
        name="prep_w_in",
    )(w_t, w_t)


def _rope_tables(s):
    inv = 1.0 / (ROPE_THETA ** (jnp.arange(0, HEAD_DIM, 2, dtype=F32) / HEAD_DIM))
    ang = jnp.arange(s, dtype=F32)[:, None] * inv[None, :]
    cos, sin = jnp.cos(ang), jnp.sin(ang)
    cos_f = jnp.concatenate([cos, cos], axis=-1)
    sin_f = jnp.concatenate([-sin, sin], axis=-1)
    tiles = lambda a: a.reshape(s // ATT_TQ, ATT_TQ, HEAD_DIM).transpose(0, 2, 1)
    return cos_f, sin_f, tiles(cos_f), tiles(sin_f)


def _layer(x2, tables, w_in_bf_all, side_weights, layer, final_norm_g,
           attn_norm_g, conv_dw_w, conv_dw_b,
           conv_ln_g, conv_ln_b, rnn_conv_w, rnn_conv_b, rglru_wa, rglru_ba, rglru_wx, rglru_bx,
           rglru_lambda, cmp_pe, cmp_k_w1, cmp_k_w2, cmp_v_w1, cmp_v_w2, mlp_norm_g):
    is_last = layer == DEPTH - 1
    proj, (w_conv_out, w_rnn_out, w_attn_out, w_o, w_mlp_up, w_mlp_down) = _in_projection(
        x2, attn_norm_g, w_in_bf_all, side_weights, layer)
    ua = _conv_branch(proj, conv_dw_w, conv_dw_b, conv_ln_g, conv_ln_b)
    ub = _rglru_branch(proj, rnn_conv_w, rnn_conv_b, rglru_wa, rglru_ba, rglru_wx, rglru_bx,
                       rglru_lambda)
    kc, vc = _compress(proj, cmp_pe, cmp_k_w1, cmp_k_w2, cmp_v_w1, cmp_v_w2)
    uc = _attention(proj, kc, vc, tables)
    x2 = _merge(ua, ub, uc, proj, x2, w_conv_out, w_rnn_out, w_attn_out, w_o)
    return _mlp(x2, mlp_norm_g, w_mlp_up, w_mlp_down, final_norm_g, is_last)


def kernel(x, attn_norm_g, w_in, conv_dw_w, conv_dw_b, conv_ln_g, conv_ln_b, w_conv_out, rnn_conv_w, rnn_conv_b, rglru_wa, rglru_ba, rglru_wx, rglru_bx, rglru_lambda, w_rnn_out, cmp_pe, cmp_k_w1, cmp_k_w2, cmp_v_w1, cmp_v_w2, w_attn_out, w_o, mlp_norm_g, w_mlp_up, w_mlp_down, final_norm_g):
    b, s, d = x.shape
    assert (b, s, d) == (BATCH, SEQ, D_MODEL)
    tables = _rope_tables(s)
    x2 = x.reshape(b * s, d)
    per_layer = (attn_norm_g, conv_dw_w, conv_dw_b, conv_ln_g, conv_ln_b,
                 rnn_conv_w, rnn_conv_b, rglru_wa, rglru_ba, rglru_wx, rglru_bx, rglru_lambda,
                 cmp_pe, cmp_k_w1, cmp_k_w2, cmp_v_w1, cmp_v_w2, mlp_norm_g)
    side_weights = (w_conv_out, w_rnn_out, w_attn_out, w_o, w_mlp_up, w_mlp_down)
    w_in_bf_all = _permute_w_in_all(w_in)
    for l in range(DEPTH):
        x2 = _layer(x2, tables, w_in_bf_all, side_weights, l, final_norm_g, *[p[l] for p in per_layer])
    return x2.reshape(b, s, d)
```

```python
import functools

import numpy as np
import jax
import jax.numpy as jnp
from jax import lax
from jax.experimental import pallas as pl
from jax.experimental.pallas import tpu as pltpu

F32 = jnp.float32
BF16 = jnp.bfloat16

D_MODEL = 2048
BATCH = 8
SEQ = 2048
DEPTH = 2

D_CONV = D_MODEL // 4
CONV_WIDTH = 31
D_RNN = 3 * D_MODEL // 8
RNN_BLOCKS = 6
RNN_BLOCK_W = D_RNN // RNN_BLOCKS
RNN_CONV_WIDTH = 4
RG_C = 8.0
N_Q_HEADS = 6
N_KV_HEADS = 2
HEAD_DIM = 128
GROUP = N_Q_HEADS // N_KV_HEADS
D_ATTN = N_Q_HEADS * HEAD_DIM
KV_W = N_KV_HEADS * HEAD_DIM
CMP_BLOCK = 32
CMP_STRIDE = 16
SEL_BLOCK = 64
SEL_TOP_N = 16
WINDOW = 512
ROPE_THETA = 10000.0
D_FF = 4 * D_MODEL
NORM_EPS = 1e-6
NEG_INF = -1e30
POS_INF = 1e30

N_CMP_PAD = SEQ // CMP_STRIDE
N_SEL = SEQ // SEL_BLOCK

LANES = 128
BF16_ROWS = 16
SLAB = 256
SL_GA, SL_GB, SL_GC = 0, 8, 16
SL_RX, SL_RG = 24, 27
SL_Q = 30
SL_KC, SL_VC, SL_KS = 33, 34, 35
SL_AV, SL_AG = 36, 38
SL_VS, SL_KW, SL_VW, SL_CG = 40, 41, 42, 43
N_SLABS = 44
N_IN_PAD = N_SLABS * SLAB

_IN_SIZES = (D_CONV, D_CONV, D_RNN, D_RNN, D_ATTN, KV_W, KV_W, KV_W, KV_W, KV_W, KV_W,
             3 * N_Q_HEADS, D_MODEL, D_MODEL, D_MODEL)
_IN_OFF = np.concatenate([[0], np.cumsum(_IN_SIZES)])
(_O_AV, _O_AG, _O_RX, _O_RG, _O_Q, _O_KC, _O_VC, _O_KS, _O_VS, _O_KW, _O_VW, _O_CG,
 _O_GA, _O_GB, _O_GC) = [int(v) for v in _IN_OFF[:-1]]
N_IN = int(_IN_OFF[-1])

VMEM_LIMIT = 62 * 1024 * 1024


def _cparams(sem, vmem=VMEM_LIMIT):
    return pltpu.CompilerParams(dimension_semantics=sem, vmem_limit_bytes=vmem)


def _sigmoid(x):
    return 0.5 * jnp.tanh(0.5 * x) + 0.5


def _gelu_tanh(x):
    c = np.float32(np.sqrt(2.0 / np.pi))
    return 0.5 * x * (1.0 + jnp.tanh(c * (x + 0.044715 * (x * x * x))))


IN_TM = 1024
IN_TN = N_IN_PAD // 4
NORM_RC = 128


def _rmsnorm_to(h_ref, x_ref, g_ref, rows):
    g = g_ref[...]

    def body(c, carry):
        r0 = pl.multiple_of(c * NORM_RC, NORM_RC)
        x = x_ref[pl.ds(r0, NORM_RC), :]
        ms = jnp.mean(x * x, axis=-1, keepdims=True)
        h_ref[pl.ds(r0, NORM_RC), :] = (x * lax.rsqrt(ms + NORM_EPS) * g).astype(h_ref.dtype)
        return carry

    lax.fori_loop(0, rows // NORM_RC, body, 0)


N_SIDE = 6


def _inproj_kernel(x_ref, g_ref, w_ref, *refs):
    side_in, o_ref, side_out, h_ref = refs[:N_SIDE], refs[N_SIDE], refs[N_SIDE + 1:-1], refs[-1]

    @pl.when(pl.program_id(1) == 0)
    def _():
        _rmsnorm_to(h_ref, x_ref, g_ref, IN_TM)

    for src, dst in zip(side_in, side_out):
        dst[...] = src[...].astype(dst.dtype)

    for k in range(IN_TN // SLAB):
        r = jnp.dot(h_ref[...], w_ref[:, k * SLAB:(k + 1) * SLAB], preferred_element_type=F32)
        o_ref[k] = r.astype(o_ref.dtype)


def _in_projection(x2, g, w_perm_all, side_weights, layer):
    rows, d = x2.shape
    n_i, n_j = rows // IN_TM, N_IN_PAD // IN_TN
    assert len(side_weights) == N_SIDE
    side_in, side_out, side_shapes = [], [], []
    for w in side_weights:
        _, r, c = w.shape
        pr = max(r // (n_i * n_j), BF16_ROWS)
        assert r % pr == 0 and pr % BF16_ROWS == 0
        piece = lambda i, j, last=r // pr - 1: jnp.minimum(i * n_j + j, last)
        side_in.append(pl.BlockSpec((None, pr, c), lambda i, j, piece=piece: (layer, piece(i, j), 0)))
        side_out.append(pl.BlockSpec((pr, c), lambda i, j, piece=piece: (piece(i, j), 0)))
        side_shapes.append(jax.ShapeDtypeStruct((r, c), BF16))
    out = pl.pallas_call(
        _inproj_kernel,
        grid=(n_i, n_j),
        in_specs=[pl.BlockSpec((IN_TM, d), lambda i, j: (i, 0)),
                  pl.BlockSpec((1, d), lambda i, j: (0, 0)),
                  pl.BlockSpec((None, d, IN_TN), lambda i, j: (layer, 0, j))] + side_in,
        out_specs=[pl.BlockSpec((IN_TN // SLAB, IN_TM, SLAB), lambda i, j: (j, i, 0))] + side_out,
        out_shape=[jax.ShapeDtypeStruct((N_SLABS, rows, SLAB), BF16)] + side_shapes,
        scratch_shapes=[pltpu.VMEM((IN_TM, d), BF16)],
        compiler_params=_cparams(("parallel", "arbitrary")),
        name="in_projection",
    )(x2, g.reshape(1, d), w_perm_all, *side_weights)
    return out[0], out[1:]


MIX_TS = 256
MIX_TR = MIX_TS * BATCH
CONV_HALO = 256
CONV_RC = 64
CONV_PARTS = 4
CONV_NORM_UNROLL = 4


def _conv_kernel(v_ref, g_ref, w_ref, b_ref, lg_ref, lb_ref, o_ref, ubuf, ybuf):
    nc = D_CONV // LANES
    per_slab = SLAB // LANES

    @pl.when(pl.program_id(0) == 0)
    def _():
        ubuf[:, 0:CONV_HALO, :] = jnp.zeros((nc, CONV_HALO, LANES), F32)

    def glu(b, carry):
        for c in range(nc):
            k, ls = c // per_slab, slice((c % per_slab) * LANES, (c % per_slab + 1) * LANES)
            v = v_ref[k, b, :, ls].astype(F32)
            g = g_ref[k, b, :, ls].astype(F32)
            ubuf[c, pl.ds(CONV_HALO + b, MIX_TS, stride=BATCH), :] = v * _sigmoid(g)
        return carry

    lax.fori_loop(0, BATCH, glu, 0)

    base = CONV_HALO - (CONV_WIDTH - 1) * BATCH

    def conv(i, carry):
        r0 = pl.multiple_of(i * CONV_RC, CONV_RC)
        for c in range(nc):
            ls = slice(c * LANES, (c + 1) * LANES)
            parts = [None] * CONV_PARTS
            for j in range(CONV_WIDTH):
                term = w_ref[j:j + 1, ls] * ubuf[c, pl.ds(r0 + base + BATCH * j, CONV_RC), :]
                k = j % CONV_PARTS
                parts[k] = term if parts[k] is None else parts[k] + term
            ybuf[c, pl.ds(r0, CONV_RC), :] = (parts[0] + parts[1]) + (parts[2] + parts[3]) + b_ref[:, ls]
        return carry

    lax.fori_loop(0, MIX_TR // CONV_RC, conv, 0)

    def norm_act(i, carry):
        for u in range(CONV_NORM_UNROLL):
            rows = pl.ds(pl.multiple_of((i * CONV_NORM_UNROLL + u) * CONV_RC, CONV_RC), CONV_RC)
            total = ybuf[0, rows, :]
            for c in range(1, nc):
                total = total + ybuf[c, rows, :]
            mu = jnp.sum(total, axis=-1, keepdims=True) * (1.0 / D_CONV)
            sq = None
            for c in range(nc):
                cen = ybuf[c, rows, :] - mu
                sq = cen * cen if sq is None else sq + cen * cen
            inv = lax.rsqrt(jnp.sum(sq, axis=-1, keepdims=True) * (1.0 / D_CONV) + NORM_EPS)
            for c in range(nc):
                ls = slice(c * LANES, (c + 1) * LANES)
                y = (ybuf[c, rows, :] - mu) * inv * lg_ref[:, ls] + lb_ref[:, ls]
                ybuf[c, rows, :] = y * _sigmoid(y)
        return carry

    lax.fori_loop(0, MIX_TR // (CONV_RC * CONV_NORM_UNROLL), norm_act, 0)

    def put(b, carry):
        for c in range(nc):
            k, ls = c // per_slab, slice((c % per_slab) * LANES, (c % per_slab + 1) * LANES)
            o_ref[k, b, :, ls] = ybuf[c, pl.ds(b, MIX_TS, stride=BATCH), :].astype(o_ref.dtype)
        return carry

    lax.fori_loop(0, BATCH, put, 0)
    ubuf[:, 0:CONV_HALO, :] = ubuf[:, MIX_TR:MIX_TR + CONV_HALO, :]


def _conv_branch(proj, w, b, lg, lb):
    seq = proj.shape[1] // BATCH
    p4 = proj.reshape(N_SLABS, BATCH, seq, SLAB)
    nk = D_CONV // SLAB
    vec = lambda: pl.BlockSpec((1, D_CONV), lambda i: (0, 0))
    out = pl.pallas_call(
        _conv_kernel,
        grid=(seq // MIX_TS,),
        in_specs=[pl.BlockSpec((nk, BATCH, MIX_TS, SLAB), lambda i: (SL_AV // nk, 0, i, 0)),
                  pl.BlockSpec((nk, BATCH, MIX_TS, SLAB), lambda i: (SL_AG // nk, 0, i, 0)),
                  pl.BlockSpec((CONV_WIDTH, D_CONV), lambda i: (0, 0)),
                  vec(), vec(), vec()],
        out_specs=pl.BlockSpec((nk, BATCH, MIX_TS, SLAB), lambda i: (0, 0, i, 0)),
        out_shape=jax.ShapeDtypeStruct((nk, BATCH, seq, SLAB), BF16),
        scratch_shapes=[pltpu.VMEM((D_CONV // LANES, CONV_HALO + MIX_TR, LANES), F32),
                        pltpu.VMEM((D_CONV // LANES, MIX_TR, LANES), F32)],
        compiler_params=_cparams(("arbitrary",)),
        name="conv_branch",
    )(p4, p4, w, b.reshape(1, -1), lg.reshape(1, -1), lb.reshape(1, -1))
    return out.reshape(nk, BATCH * seq, SLAB)


RNN_HALO = 32
RNN_RC = 256


def _rglru_kernel(x_ref, gate_ref, cw_ref, cb_ref, wa_ref, ba_ref, wx_ref, bx_ref, lam_ref,
                  o_ref, xbuf, abuf, gbuf, hstate):
    per_slab = SLAB // RNN_BLOCK_W

    @pl.when(pl.program_id(0) == 0)
    def _():
        xbuf[:, 0:RNN_HALO, :] = jnp.zeros((RNN_BLOCKS, RNN_HALO, RNN_BLOCK_W), F32)
        hstate[...] = jnp.zeros_like(hstate)

    def slab_cols(n):
        return n // per_slab, slice((n % per_slab) * RNN_BLOCK_W, (n % per_slab + 1) * RNN_BLOCK_W)

    def load(b, carry):
        for n in range(RNN_BLOCKS):
            k, ls = slab_cols(n)
            xbuf[n, pl.ds(RNN_HALO + b, MIX_TS, stride=BATCH), :] = x_ref[k, b, :, ls].astype(F32)
        return carry

    lax.fori_loop(0, BATCH, load, 0)

    z = -lam_ref[...]
    softplus = jnp.maximum(z, 0.0) + jnp.log(1.0 + jnp.exp(-jnp.abs(z)))
    coef = -RG_C * softplus
    base = RNN_HALO - (RNN_CONV_WIDTH - 1) * BATCH

    def gates(i, carry):
        r0 = pl.multiple_of(i * RNN_RC, RNN_RC)
        for n in range(RNN_BLOCKS):
            cs = slice(n * RNN_BLOCK_W, (n + 1) * RNN_BLOCK_W)
            y = jnp.zeros((RNN_RC, RNN_BLOCK_W), F32)
            for j in range(RNN_CONV_WIDTH):
                y = y + cw_ref[j:j + 1, cs] * xbuf[n, pl.ds(r0 + base + BATCH * j, RNN_RC), :]
            y = y + cb_ref[:, cs]
            yb = y.astype(BF16)
            ra = _sigmoid(jnp.dot(yb, wa_ref[n], preferred_element_type=F32) + ba_ref[:, cs])
            ri = _sigmoid(jnp.dot(yb, wx_ref[n], preferred_element_type=F32) + bx_ref[:, cs])
            a = jnp.exp(coef[:, cs] * ra)
            abuf[n, pl.ds(r0, RNN_RC), :] = a
            gbuf[n, pl.ds(r0, RNN_RC), :] = jnp.sqrt(1.0 - a * a) * (ri * y)
        return carry

    lax.fori_loop(0, MIX_TR // RNN_RC, gates, 0)

    def step(t, h):
        r0 = pl.multiple_of(t * BATCH, BATCH)
        h = abuf[:, pl.ds(r0, BATCH), :] * h + gbuf[:, pl.ds(r0, BATCH), :]
        gbuf[:, pl.ds(r0, BATCH), :] = h
        return h

    hstate[...] = lax.fori_loop(0, MIX_TS, step, hstate[...], unroll=8)

    def put(b, carry):
        for n in range(RNN_BLOCKS):
            k, ls = slab_cols(n)
            h = gbuf[n, pl.ds(b, MIX_TS, stride=BATCH), :]
            o_ref[k, b, :, ls] = (h * _gelu_tanh(gate_ref[k, b, :, ls].astype(F32))).astype(o_ref.dtype)
        return carry

    lax.fori_loop(0, BATCH, put, 0)
    xbuf[:, 0:RNN_HALO, :] = xbuf[:, MIX_TR:MIX_TR + RNN_HALO, :]


def _rglru_branch(proj, cw, cb, wa, ba, wx, bx, lam):
    seq = proj.shape[1] // BATCH
    p4 = proj.reshape(N_SLABS, BATCH, seq, SLAB)
    nk = D_RNN // SLAB
    vec = lambda: pl.BlockSpec((1, D_RNN), lambda i: (0, 0))
    blk = lambda: pl.BlockSpec((RNN_BLOCKS, RNN_BLOCK_W, RNN_BLOCK_W), lambda i: (0, 0, 0))
    out = pl.pallas_call(
        _rglru_kernel,
        grid=(seq // MIX_TS,),
        in_specs=[pl.BlockSpec((nk, BATCH, MIX_TS, SLAB), lambda i: (SL_RX // nk, 0, i, 0)),
                  pl.BlockSpec((nk, BATCH, MIX_TS, SLAB), lambda i: (SL_RG // nk, 0, i, 0)),
                  pl.BlockSpec((RNN_CONV_WIDTH, D_RNN), lambda i: (0, 0)),
                  vec(), blk(), vec(), blk(), vec(), vec()],
        out_specs=pl.BlockSpec((nk, BATCH, MIX_TS, SLAB), lambda i: (0, 0, i, 0)),
        out_shape=jax.ShapeDtypeStruct((nk, BATCH, seq, SLAB), BF16),
        scratch_shapes=[pltpu.VMEM((RNN_BLOCKS, RNN_HALO + MIX_TR, RNN_BLOCK_W), F32),
                        pltpu.VMEM((RNN_BLOCKS, MIX_TR, RNN_BLOCK_W), F32),
                        pltpu.VMEM((RNN_BLOCKS, MIX_TR, RNN_BLOCK_W), F32),
                        pltpu.VMEM((RNN_BLOCKS, BATCH, RNN_BLOCK_W), F32)],
        compiler_params=_cparams(("arbitrary",)),
        name="rglru_branch",
    )(p4, p4, cw, cb.reshape(1, -1), wa.astype(BF16), ba.reshape(1, -1),
      wx.astype(BF16), bx.reshape(1, -1), lam.reshape(1, -1))
    return out.reshape(nk, BATCH * seq, SLAB)


def _compress_kernel(k_ref, v_ref, pe_ref, kw1_ref, vw1_ref, kw2_ref, vw2_ref, kc_ref, vc_ref, stage):
    half = CMP_STRIDE * HEAD_DIM
    pe = pe_ref[...].astype(BF16)
    for src, w1_ref, w2_ref, dst in ((k_ref, kw1_ref, kw2_ref, kc_ref),
                                     (v_ref, vw1_ref, vw2_ref, vc_ref)):
        w1 = w1_ref[...]
        pe_term = (jnp.dot(pe[:, :half], w1[:, :HEAD_DIM], preferred_element_type=F32)
                   + jnp.dot(pe[:, half:], w1[:, HEAD_DIM:], preferred_element_type=F32))
        for h in range(N_KV_HEADS):
            stage[...] = src[:, h * HEAD_DIM:(h + 1) * HEAD_DIM].astype(F32)
            x = jnp.concatenate([stage[pl.ds(l, N_CMP_PAD, stride=CMP_STRIDE), :].astype(BF16)
                                 for l in range(CMP_STRIDE)], axis=1)
            p = jnp.dot(x, w1, preferred_element_type=F32)
            hi_next = pltpu.roll(p[:, HEAD_DIM:], N_CMP_PAD - 1, 0)
            pre = p[:, :HEAD_DIM] + hi_next + pe_term[0:1, :]
            y = jnp.dot(_gelu_tanh(pre).astype(BF16), w2_ref[...], preferred_element_type=F32)
            dst[:, h * HEAD_DIM:(h + 1) * HEAD_DIM] = y.astype(dst.dtype)


def _compress(proj, pe, kw1, kw2, vw1, vw2):
    seq = proj.shape[1] // BATCH
    half = CMP_STRIDE * HEAD_DIM
    pe2 = jnp.zeros((8, CMP_BLOCK * HEAD_DIM), F32).at[0].set(pe.reshape(-1))

    def w1cat(w1):
        return jnp.concatenate([w1[:half], w1[half:]], axis=-1).astype(BF16)

    kv_spec = lambda sl: pl.BlockSpec((None, seq, SLAB), lambda b: (sl, b, 0))
    w1_spec = lambda: pl.BlockSpec((half, 2 * HEAD_DIM), lambda b: (0, 0))
    w2_spec = lambda: pl.BlockSpec((HEAD_DIM, HEAD_DIM), lambda b: (0, 0))
    out_spec = lambda: pl.BlockSpec((None, N_CMP_PAD, KV_W), lambda b: (b, 0, 0))
    return pl.pallas_call(
        _compress_kernel,
        grid=(BATCH,),
        in_specs=[kv_spec(SL_KC), kv_spec(SL_VC),
                  pl.BlockSpec((8, CMP_BLOCK * HEAD_DIM), lambda b: (0, 0)),
                  w1_spec(), w1_spec(), w2_spec(), w2_spec()],
        out_specs=[out_spec(), out_spec()],
        out_shape=[jax.ShapeDtypeStruct((BATCH, N_CMP_PAD, KV_W), BF16)] * 2,
        scratch_shapes=[pltpu.VMEM((seq, HEAD_DIM), F32)],
        compiler_params=_cparams(("parallel",)),
        name="compress_kv",
    )(proj, proj, pe2, w1cat(kw1), w1cat(vw1), kw2.astype(BF16), vw2.astype(BF16))


ATT_TQ = 256
ATT_TK = 256
ATT_NKB = SEQ // ATT_TK
ATT_WB = WINDOW // ATT_TK + 1
ATT_TS = 2 * ATT_TK
ATT_LANES = GROUP * ATT_TQ


def _tile_heads(x):
    return jnp.concatenate([x] * GROUP, axis=1)


def _attn_kernel(q_ref, ks_ref, vs_ref, kw_ref, vw_ref, kc_ref, vc_ref, cg_ref, cos_ref, sin_ref,
                 cost_ref, sint_ref, ovt_ref, o_ref, ksr, kwr, vst, vwt, vct, qaug, score_scr):
    qi = pl.program_id(1)
    scale = np.float32(HEAD_DIM ** -0.5 * np.log2(np.e))
    half = HEAD_DIM // 2

    @pl.when(qi == 0)
    def _():
        def prep(c, carry):
            r0 = pl.multiple_of(c * ATT_TK, ATT_TK)
            cos = cos_ref[pl.ds(r0, ATT_TK), :]
            sin = sin_ref[pl.ds(r0, ATT_TK), :]
            blk = (r0 + lax.broadcasted_iota(jnp.int32, (ATT_TK, HEAD_DIM), 0)) >> 6
            onehot = jnp.where(lax.broadcasted_iota(jnp.int32, (ATT_TK, HEAD_DIM), 1) == blk, 1.0, 0.0)
            for h in range(N_KV_HEADS):
                cs = slice(h * HEAD_DIM, (h + 1) * HEAD_DIM)
                x = ks_ref[pl.ds(r0, ATT_TK), cs].astype(F32)
                ksr[h, pl.ds(r0, ATT_TK), 0:HEAD_DIM] = (
                    x * cos + pltpu.roll(x, half, 1) * sin).astype(ksr.dtype)
                ksr[h, pl.ds(r0, ATT_TK), HEAD_DIM:2 * HEAD_DIM] = onehot.astype(ksr.dtype)
                x = kw_ref[pl.ds(r0, ATT_TK), cs].astype(F32)
                kwr[pl.ds(r0, ATT_TK), cs] = (x * cos + pltpu.roll(x, half, 1) * sin).astype(kwr.dtype)
                vwt[h, c] = vw_ref[pl.ds(r0, ATT_TK), cs].astype(F32).T.astype(vwt.dtype)
            return carry

        lax.fori_loop(0, ATT_NKB, prep, 0)

        def prep_vs(c, carry):
            for h in range(N_KV_HEADS):
                cs = slice(h * HEAD_DIM, (h + 1) * HEAD_DIM)
                parts = []
                for i in range(ATT_TS // ATT_TK):
                    r0 = pl.multiple_of(c * ATT_TS + i * ATT_TK, ATT_TK)
                    parts.append(vs_ref[pl.ds(r0, ATT_TK), cs].astype(F32).T.astype(vst.dtype))
                vst[h, c] = jnp.concatenate(parts, axis=1)
            return carry

        lax.fori_loop(0, SEQ // ATT_TS, prep_vs, 0)
        for h in range(N_KV_HEADS):
            vct[h] = vc_ref[:, h * HEAD_DIM:(h + 1) * HEAD_DIM].astype(F32).T.astype(vct.dtype)
        qaug[:, HEAD_DIM + N_SEL:, :] = jnp.zeros(
            (N_KV_HEADS, HEAD_DIM - N_SEL, ATT_LANES), qaug.dtype)

    t0 = qi * ATT_TQ
    cos_t = cost_ref[...]
    sin_t = sint_ref[...]
    gate_t = _sigmoid(cg_ref[:, 0:HEAD_DIM].astype(F32).T)

    t_c = t0 + (lax.broadcasted_iota(jnp.int32, (N_CMP_PAD, ATT_LANES), 1) & (ATT_TQ - 1))
    n_sub = lax.broadcasted_iota(jnp.int32, (N_CMP_PAD, ATT_LANES), 0)
    valid_c = (n_sub * CMP_STRIDE + (CMP_BLOCK - 1)) <= t_c
    t_s = t0 + lax.broadcasted_iota(jnp.int32, (N_SEL, ATT_TQ), 1)
    m_sub = lax.broadcasted_iota(jnp.int32, (N_SEL, ATT_TQ), 0)
    cur = t_s >> 6
    valid_s = m_sub <= cur
    forced = (m_sub == 0) | (m_sub == cur) | (m_sub == cur - 1)
    w_blk = jnp.maximum(qi - WINDOW // ATT_TK, 0)
    w_start = pl.multiple_of(w_blk * ATT_TK, ATT_TK)
    dist = ((t0 + lax.broadcasted_iota(jnp.int32, (ATT_WB * ATT_TK, ATT_TQ), 1))
            - (w_start + lax.broadcasted_iota(jnp.int32, (ATT_WB * ATT_TK, ATT_TQ), 0)))
    win_bias = jnp.where((dist >= 0) & (dist < WINDOW), 0.0, NEG_INF)

    heads = range(N_KV_HEADS)
    head_cols = [slice(h * HEAD_DIM, (h + 1) * HEAD_DIM) for h in heads]
    qn3 = []
    for h in heads:
        q_plain, q_rot = [], []
        for g in range(GROUP):
            hq = h * GROUP + g
            xt = q_ref[hq // 2, :, (hq % 2) * HEAD_DIM:(hq % 2 + 1) * HEAD_DIM].astype(F32).T
            swapped = jnp.concatenate([xt[half:], xt[:half]], axis=0)
            q_plain.append((xt * scale).astype(BF16))
            q_rot.append(((xt * cos_t + swapped * sin_t) * scale).astype(BF16))
        qn3.append(jnp.concatenate(q_plain, axis=1))
        qaug[h, 0:HEAD_DIM, :] = jnp.concatenate(q_rot, axis=1)

    s_cmp = [jnp.where(valid_c, jnp.dot(kc_ref[:, head_cols[h]], qn3[h], preferred_element_type=F32),
                       NEG_INF) for h in heads]
    win_bias3 = _tile_heads(win_bias)
    s_win = [jnp.dot(kwr[pl.ds(w_start, ATT_WB * ATT_TK), head_cols[h]], qaug[h, 0:HEAD_DIM, :],
                     preferred_element_type=F32) + win_bias3 for h in heads]
    p_cmp = []
    for h in heads:
        mx = jnp.max(s_cmp[h], axis=0, keepdims=True)
        e = jnp.where(valid_c, jnp.exp2(s_cmp[h] - mx), 0.0)
        den = jnp.sum(e, axis=0, keepdims=True)
        p_cmp.append(e / jnp.where(den > 0.0, den, 1.0))
    p_win, den_win = [], []
    for h in heads:
        pw = jnp.exp2(s_win[h] - jnp.max(s_win[h], axis=0, keepdims=True))
        den_win.append(jnp.sum(pw, axis=0, keepdims=True))
        p_win.append(pw.astype(BF16))
    o_cmp = [jnp.dot(vct[h], p_cmp[h].astype(BF16), preferred_element_type=F32) for h in heads]
    for h in heads:
        p_sum = p_cmp[h][:, 0:ATT_TQ]
        for g in range(1, GROUP):
            p_sum = p_sum + p_cmp[h][:, g * ATT_TQ:(g + 1) * ATT_TQ]
        imp = jnp.dot(ovt_ref[...], p_sum, preferred_element_type=F32,
                      precision=lax.Precision.HIGHEST)
        score_scr[h] = jnp.where(valid_s, jnp.where(forced, POS_INF, imp), NEG_INF)
    o_win = []
    for h in heads:
        acc_w = None
        for i in range(ATT_WB):
            part = jnp.dot(vwt[h, w_blk + i], p_win[h][i * ATT_TK:(i + 1) * ATT_TK],
                           preferred_element_type=F32)
            acc_w = part if acc_w is None else acc_w + part
        o_win.append(acc_w / den_win[h])
    m_v = lax.broadcasted_iota(jnp.int32, (8, ATT_TQ), 0)
    sel_rows = [[] for _ in heads]
    for v in range(N_SEL // 8):
        sc_v = [score_scr[h, 8 * v:8 * v + 8, :] for h in heads]
        cnt = [jnp.zeros((8, ATT_TQ), F32) for _ in heads]
        for mp in range(N_SEL):
            for h in heads:
                row = score_scr[h, mp:mp + 1, :]
                if mp < 8 * v:
                    beats = row >= sc_v[h]
                elif mp >= 8 * v + 8:
                    beats = row > sc_v[h]
                else:
                    beats = (row > sc_v[h]) | ((row == sc_v[h]) & (m_v > mp - 8 * v))
                cnt[h] = cnt[h] + jnp.where(beats, 1.0, 0.0)
        for h in heads:
            keep = (cnt[h] < SEL_TOP_N) & (sc_v[h] > NEG_INF)
            sel_rows[h].append(jnp.where(keep, 0.0, NEG_INF))
    for h in heads:
        sel_bias = jnp.concatenate(sel_rows[h], axis=0)
        qaug[h, HEAD_DIM:HEAD_DIM + N_SEL, :] = _tile_heads(sel_bias).astype(qaug.dtype)

    def sel_step(j, carry, causal):
        k0 = pl.multiple_of(j * ATT_TS, ATT_TS)
        if causal:
            kpos = k0 + lax.broadcasted_iota(jnp.int32, (ATT_TS, ATT_TQ), 0)
            tpos = t0 + lax.broadcasted_iota(jnp.int32, (ATT_TS, ATT_TQ), 1)
            causal_bias = _tile_heads(jnp.where(kpos <= tpos, 0.0, NEG_INF))
        out = []
        for h in range(N_KV_HEADS):
            m_run, l_run, acc = carry[h]
            sc = jnp.dot(ksr[h, pl.ds(k0, ATT_TS), :], qaug[h], preferred_element_type=F32)
            if causal:
                sc = sc + causal_bias
            m_new = jnp.maximum(m_run, jnp.max(sc, axis=0, keepdims=True))
            alpha = jnp.exp2(m_run - m_new)
            pr = jnp.exp2(sc - m_new)
            l_new = alpha * l_run + jnp.sum(pr, axis=0, keepdims=True)
            acc = alpha * acc + jnp.dot(vst[h, j], pr.astype(BF16), preferred_element_type=F32)
            out.append((m_new, l_new, acc))
        return tuple(out)

    init = (jnp.full((1, ATT_LANES), NEG_INF, F32), jnp.zeros((1, ATT_LANES), F32),
            jnp.zeros((HEAD_DIM, ATT_LANES), F32))
    last = (t0 + ATT_TQ - 1) // ATT_TS
    carry = sel_step(last, (init,) * N_KV_HEADS, True)
    sel_out = lax.fori_loop(0, last, lambda j, c: sel_step(j, c, False), carry)

    for h in heads:
        _, l_s, acc_s = sel_out[h]
        o_sel = acc_s / l_s
        for g in range(GROUP):
            hq = h * GROUP + g
            ls = slice(g * ATT_TQ, (g + 1) * ATT_TQ)
            out_t = (gate_t[3 * hq:3 * hq + 1, :] * o_cmp[h][:, ls]
                     + gate_t[3 * hq + 1:3 * hq + 2, :] * o_sel[:, ls]
                     + gate_t[3 * hq + 2:3 * hq + 3, :] * o_win[h][:, ls])
            o_ref[hq // 2, :, (hq % 2) * HEAD_DIM:(hq % 2 + 1) * HEAD_DIM] = out_t.T.astype(o_ref.dtype)


def _attention(proj, kc, vc, tables):
    cos, sin, cos_t, sin_t = tables
    rows = proj.shape[1]
    seq = rows // BATCH
    nq = seq // ATT_TQ
    c_start = np.arange(N_CMP_PAD) * CMP_STRIDE
    s_start = np.arange(N_SEL) * SEL_BLOCK
    ovt = ((c_start[None, :] < s_start[:, None] + SEL_BLOCK)
           & (c_start[None, :] + CMP_BLOCK > s_start[:, None])
           & (np.arange(N_CMP_PAD)[None, :] < N_CMP_PAD - 1)).astype(np.float32)

    full = lambda sl: pl.BlockSpec((None, seq, SLAB), lambda b, qi: (sl, b, 0))
    cmp_spec = lambda: pl.BlockSpec((None, N_CMP_PAD, KV_W), lambda b, qi: (b, 0, 0))
    tab = lambda: pl.BlockSpec((seq, HEAD_DIM), lambda b, qi: (0, 0))
    tab_t = lambda: pl.BlockSpec((None, HEAD_DIM, ATT_TQ), lambda b, qi: (qi, 0, 0))
    nq3 = D_ATTN // SLAB
    kv_t = lambda tk: pltpu.VMEM((N_KV_HEADS, seq // tk, HEAD_DIM, tk), BF16)
    return pl.pallas_call(
        _attn_kernel,
        grid=(BATCH, nq),
        in_specs=[pl.BlockSpec((nq3, ATT_TQ, SLAB), lambda b, qi: (SL_Q // nq3, b * nq + qi, 0)),
                  full(SL_KS), full(SL_VS), full(SL_KW), full(SL_VW),
                  cmp_spec(), cmp_spec(),
                  pl.BlockSpec((None, ATT_TQ, SLAB), lambda b, qi: (SL_CG, b * nq + qi, 0)),
                  tab(), tab(), tab_t(), tab_t(),
                  pl.BlockSpec((N_SEL, N_CMP_PAD), lambda b, qi: (0, 0))],
        out_specs=pl.BlockSpec((nq3, ATT_TQ, SLAB), lambda b, qi: (0, b * nq + qi, 0)),
        out_shape=jax.ShapeDtypeStruct((nq3, rows, SLAB), BF16),
        scratch_shapes=[pltpu.VMEM((N_KV_HEADS, seq, 2 * HEAD_DIM), BF16),
                        pltpu.VMEM((seq, KV_W), BF16),
                        kv_t(ATT_TS), kv_t(ATT_TK),
                        pltpu.VMEM((N_KV_HEADS, HEAD_DIM, N_CMP_PAD), BF16),
                        pltpu.VMEM((N_KV_HEADS, 2 * HEAD_DIM, ATT_LANES), BF16),
                        pltpu.VMEM((N_KV_HEADS, N_SEL, ATT_TQ), F32)],
        compiler_params=_cparams(("parallel", "arbitrary")),
        name="sparse_attention",
    )(proj, proj, proj, proj, proj, kc, vc, proj, cos, sin, cos_t, sin_t, jnp.asarray(ovt))


MERGE_TM = 512
MERGE_TN = 512


def _cat_slabs(ref, first, n):
    return jnp.concatenate([ref[first + k] for k in range(n)], axis=-1)


def _merge_kernel(ua_ref, ub_ref, uc_ref, ga_ref, gb_ref, gc_ref, x_ref,
                  wa_ref, wb_ref, wc_ref, wo_ref, o_ref):
    per = MERGE_TN // SLAB
    branches = ((ua_ref, ga_ref, wa_ref), (ub_ref, gb_ref, wb_ref), (uc_ref, gc_ref, wc_ref))
    acts = [_cat_slabs(u_ref, 0, u_ref.shape[0]) for u_ref, _, _ in branches]
    for c in range(D_MODEL // MERGE_TN):
        cs = slice(c * MERGE_TN, (c + 1) * MERGE_TN)
        y = None
        for u, (_, g_ref, w_ref) in zip(acts, branches):
            p = jnp.dot(u, w_ref[:, cs], preferred_element_type=F32)
            term = _sigmoid(_cat_slabs(g_ref, c * per, per).astype(F32)) * p
            y = term if y is None else y + term
        part = jnp.dot(y.astype(BF16), wo_ref[cs, :], preferred_element_type=F32)
        if c == 0:
            o_ref[...] = x_ref[...] + part
        else:
            o_ref[...] += part


def _merge(ua, ub, uc, proj, x2, wa, wb, wc, wo):
    rows, d = x2.shape
    ng = d // SLAB
    act = lambda n: pl.BlockSpec((n, MERGE_TM, SLAB), lambda i: (0, i, 0))
    gate = lambda sl: pl.BlockSpec((ng, MERGE_TM, SLAB), lambda i: (sl // ng, i, 0))
    res = lambda k: pl.BlockSpec((k, d), lambda i: (0, 0), pipeline_mode=pl.Buffered(1))
    return pl.pallas_call(
        _merge_kernel,
        grid=(rows // MERGE_TM,),
        in_specs=[act(ua.shape[0]), act(ub.shape[0]), act(uc.shape[0]),
                  gate(SL_GA), gate(SL_GB), gate(SL_GC),
                  pl.BlockSpec((MERGE_TM, d), lambda i: (i, 0)),
                  res(wa.shape[0]), res(wb.shape[0]), res(wc.shape[0]), res(wo.shape[0])],
        out_specs=pl.BlockSpec((MERGE_TM, d), lambda i: (i, 0)),
        out_shape=jax.ShapeDtypeStruct((rows, d), F32),
        compiler_params=_cparams(("parallel",)),
        name="merge_out_proj",
    )(ua, ub, uc, proj, proj, proj, x2, wa, wb, wc, wo)


MLP_TM = 1024
MLP_TF = 1024


def _mlp_kernel(x_ref, g_ref, wu_ref, wd_ref, og_ref, o_ref, h_ref, *, norm_output):
    @pl.when(pl.program_id(1) == 0)
    def _():
        _rmsnorm_to(h_ref, x_ref, g_ref, MLP_TM)
        o_ref[...] = x_ref[...]

    a = jnp.dot(h_ref[...], wu_ref[...], preferred_element_type=F32)
    a = jnp.maximum(a, 0.0)
    o_ref[...] += jnp.dot((a * a).astype(BF16), wd_ref[...], preferred_element_type=F32)

    if norm_output:
        @pl.when(pl.program_id(1) == pl.num_programs(1) - 1)
        def _():
            _rmsnorm_to(o_ref, o_ref, og_ref, MLP_TM)


def _mlp(x2, g, wu, wd, out_g, norm_output):
    rows, d = x2.shape
    f = wu.shape[1]
    return pl.pallas_call(
        functools.partial(_mlp_kernel, norm_output=norm_output),
        grid=(rows // MLP_TM, f // MLP_TF),
        in_specs=[pl.BlockSpec((MLP_TM, d), lambda i, j: (i, 0)),
                  pl.BlockSpec((1, d), lambda i, j: (0, 0)),
                  pl.BlockSpec((d, MLP_TF), lambda i, j: (0, j)),
                  pl.BlockSpec((MLP_TF, d), lambda i, j: (j, 0)),
                  pl.BlockSpec((1, d), lambda i, j: (0, 0))],
        out_specs=pl.BlockSpec((MLP_TM, d), lambda i, j: (i, 0)),
        out_shape=jax.ShapeDtypeStruct((rows, d), F32),
        scratch_shapes=[pltpu.VMEM((MLP_TM, d), BF16)],
        compiler_params=_cparams(("parallel", "arbitrary")),
        name="mlp",
    )(x2, g.reshape(1, d), wu, wd, out_g.reshape(1, d))


def _prep_w_in_kernel(a_ref, b_ref, o_ref):
    j = pl.program_id(0)
    shift = _O_GA % SLAB
    n_gate = 3 * N_Q_HEADS
    n_layers = a_ref.shape[1]

    @pl.when(j < SL_RX)
    def _():
        for l in range(n_layers):
            at = a_ref[:, l, :].T
            bt = b_ref[:, l, :].T
            o_ref[l] = jnp.concatenate([at[:, shift:], bt[:, :shift]], axis=1).astype(o_ref.dtype)

    @pl.when((j >= SL_RX) & (j < SL_CG))
    def _():
        for l in range(n_layers):
            o_ref[l] = a_ref[:, l, :].T.astype(o_ref.dtype)

    @pl.when(j == SL_CG)
    def _():
        lane = lax.broadcasted_iota(jnp.int32, (D_MODEL, SLAB), 1)
        for l in range(n_layers):
            o_ref[l] = jnp.where(lane < n_gate, a_ref[:, l, :].T, 0.0).astype(o_ref.dtype)


def _permute_w_in_all(w_all):
    n_layers, d, n_in = w_all.shape
    w_t = jnp.transpose(w_all, (2, 0, 1))

    def src_block(j):
        blk = (_O_GA // SLAB) + j
        blk = jnp.where(j >= SL_RX, _O_RX // SLAB + (j - SL_RX), blk)
        blk = jnp.where(j >= SL_Q, _O_Q // SLAB + (j - SL_Q), blk)
        blk = jnp.where(j >= SL_AV, _O_AV // SLAB + (j - SL_AV), blk)
        blk = jnp.where(j >= SL_VS, _O_VS // SLAB + (j - SL_VS), blk)
        return jnp.where(j >= SL_CG, _O_CG // SLAB, blk)

    tail_rows = SLAB // 2
    assert _O_GA % SLAB <= tail_rows
    last_tail = (n_in - 1) // tail_rows
    return pl.pallas_call(
        _prep_w_in_kernel,
        grid=(N_SLABS,),
        in_specs=[pl.BlockSpec((SLAB, n_layers, d), lambda j: (src_block(j), 0, 0)),
                  pl.BlockSpec((tail_rows, n_layers, d),
                               lambda j: (jnp.where(j < SL_RX, 2 * (src_block(j) + 1), last_tail), 0, 0))],
        out_specs=pl.BlockSpec((n_layers, d, SLAB), lambda j: (0, 0, j)),
        out_shape=jax.ShapeDtypeStruct((n_layers, d, N_IN_PAD), BF16),
        compiler_params=_cparams(("parallel",)),
        name="prep_w_in",
    )(w_t, w_t)


def _rope_tables(s):
    inv = 1.0 / (ROPE_THETA ** (jnp.arange(0, HEAD_DIM, 2, dtype=F32) / HEAD_DIM))
    ang = jnp.arange(s, dtype=F32)[:, None] * inv[None, :]
    cos, sin = jnp.cos(ang), jnp.sin(ang)
    cos_f = jnp.concatenate([cos, cos], axis=-1)
    sin_f = jnp.concatenate([-sin, sin], axis=-1)
    tiles = lambda a: a.reshape(s // ATT_TQ, ATT_TQ, HEAD_DIM).transpose(0, 2, 1)
    return cos_f, sin_f, tiles(cos_f), tiles(sin_f)


def _layer(x2, tables, w_in_bf_all, side_weights, layer, final_norm_g,
           attn_norm_g, conv_dw_w, conv_dw_b,
           conv_ln_g, conv_ln_b, rnn_conv_w, rnn_conv_b, rglru_wa, rglru_ba, rglru_wx, rglru_bx,
           rglru_lambda, cmp_pe, cmp_k_w1, cmp_k_w2, cmp_v_w1, cmp_v_w2, mlp_norm_g):
    is_last = layer == DEPTH - 1
    proj, (w_conv_out, w_rnn_out, w_attn_out, w_o, w_mlp_up, w_mlp_down) = _in_projection(
        x2, attn_norm_g, w_in_bf_all, side_weights, layer)
    ua = _conv_branch(proj, conv_dw_w, conv_dw_b, conv_ln_g, conv_ln_b)
    ub = _rglru_branch(proj, rnn_conv_w, rnn_conv_b, rglru_wa, rglru_ba, rglru_wx, rglru_bx,
                       rglru_lambda)
    kc, vc = _compress(proj, cmp_pe, cmp_k_w1, cmp_k_w2, cmp_v_w1, cmp_v_w2)
    uc = _attention(proj, kc, vc, tables)
    x2 = _merge(ua, ub, uc, proj, x2, w_conv_out, w_rnn_out, w_attn_out, w_o)
    return _mlp(x2, mlp_norm_g, w_mlp_up, w_mlp_down, final_norm_g, is_last)


def kernel(x, attn_norm_g, w_in, conv_dw_w, conv_dw_b, conv_ln_g, conv_ln_b, w_conv_out, rnn_conv_w, rnn_conv_b, rglru_wa, rglru_ba, rglru_wx, rglru_bx, rglru_lambda, w_rnn_out, cmp_pe, cmp_k_w1, cmp_k_w2, cmp_v_w1, cmp_v_w2, w_attn_out, w_o, mlp_norm_g, w_mlp_up, w_mlp_down, final_norm_g):
    b, s, d = x.shape
    assert (b, s, d) == (BATCH, SEQ, D_MODEL)
    tables = _rope_tables(s)
    x2 = x.reshape(b * s, d)
    per_layer = (attn_norm_g, conv_dw_w, conv_dw_b, conv_ln_g, conv_ln_b,
                 rnn_conv_w, rnn_conv_b, rglru_wa, rglru_ba, rglru_wx, rglru_bx, rglru_lambda,
                 cmp_pe, cmp_k_w1, cmp_k_w2, cmp_v_w1, cmp_v_w2, mlp_norm_g)
    side_weights = (w_conv_out, w_rnn_out, w_attn_out, w_o, w_mlp_up, w_mlp_down)
    w_in_bf_all = _permute_w_in_all(w_in)
    for l in range(DEPTH):
        x2 = _layer(x2, tables, w_in_bf_all, side_weights, l, final_norm_g, *[p[l] for p in per_layer])
    return x2.reshape(b, s, d)
```
